```python
import jax, jax.numpy as jnp
from jax import lax
import numpy as np

D_MODEL = 1024
BATCH = 16
SEQ = 2048
DEPTH = 2

HEAD_DIM = 64
N_GROUPS = 4
GROUP_WIDTH = D_MODEL // N_GROUPS
GROUP_HEADS = GROUP_WIDTH // HEAD_DIM
MIX_WIDTH = N_GROUPS * GROUP_WIDTH
IN_COLS = 9 * GROUP_WIDTH
MOBA_BLOCK = 256
MOBA_TOPK = 3
MOBA_QCHUNK = 16
DILATED_CFGS = ((128, 1), (512, 4), (2048, 16))
BAND = 128
CONV_WIDTH = 31
FFN_CONV_WIDTH = 3
D_FF = 2816
N_MEM = 256
ROPE_THETA = 10000.0
EPS = 1e-6
ATTN_SCALE = HEAD_DIM ** -0.5

kernel_name = 'hybrid_moba_dilated_conformer_block'


def rms_norm(x, g):
    xf = x.astype(jnp.float32)
    y = xf * lax.rsqrt(jnp.mean(xf * xf, axis=-1, keepdims=True) + EPS)
    return (y * g.astype(jnp.float32)).astype(x.dtype)


def layer_norm(x, g, b):
    xf = x.astype(jnp.float32)
    mu = jnp.mean(xf, axis=-1, keepdims=True)
    var = jnp.mean(jnp.square(xf - mu), axis=-1, keepdims=True)
    y = (xf - mu) * lax.rsqrt(var + EPS)
    return (y * g.astype(jnp.float32) + b.astype(jnp.float32)).astype(x.dtype)


def rope_tables(seq):
    inv = ROPE_THETA ** (-jnp.arange(0, HEAD_DIM, 2, dtype=jnp.float32) / HEAD_DIM)
    ang = jnp.arange(seq, dtype=jnp.float32)[:, None] * inv[None, :]
    ang = jnp.concatenate([ang, ang], axis=-1)
    return jnp.cos(ang), jnp.sin(ang)


def apply_rope(x, cos, sin):
    half = HEAD_DIM // 2
    xf = x.astype(jnp.float32)
    rot = jnp.concatenate([-xf[..., half:], xf[..., :half]], axis=-1)
    return (xf * cos + rot * sin).astype(x.dtype)


def split_heads(t, n_heads):
    b, s, _ = t.shape
    return t.reshape(b, s, n_heads, HEAD_DIM).transpose(0, 2, 1, 3)


def merge_heads(t):
    b, h, s, d = t.shape
    return t.transpose(0, 2, 1, 3).reshape(b, s, h * d)


def causal_depthwise_conv(x, w, b):
    k, c = w.shape
    y = lax.conv_general_dilated(x, w[:, None, :].astype(x.dtype), window_strides=(1,),
                                 padding=[(k - 1, 0)], dimension_numbers=('NWC', 'WIO', 'NWC'),
                                 feature_group_count=c)
    return y + b.astype(x.dtype)


def moba_attention(q, k, v):
    b, h, s, d = q.shape
    nb = -(-s // MOBA_BLOCK)
    sp = nb * MOBA_BLOCK
    pad = ((0, 0), (0, 0), (0, sp - s), (0, 0))
    kb = jnp.pad(k, pad).reshape(b, h, nb, MOBA_BLOCK, d)
    vb = jnp.pad(v, pad).reshape(b, h, nb, MOBA_BLOCK, d)
    kmean = jnp.mean(kb.astype(jnp.float32), axis=3)
    qpos = jnp.arange(s)
    qblk = qpos // MOBA_BLOCK
    gate = jnp.einsum('bhsd,bhnd->bhsn', q.astype(jnp.float32), kmean)
    fully_past = jnp.arange(nb)[None, :] < qblk[:, None]
    gate = jnp.where(fully_past, gate, -jnp.inf)
    kk = min(MOBA_TOPK, nb)
    _, top_idx = lax.top_k(gate, kk)
    own = jnp.broadcast_to(qblk[None, None, :, None], (b, h, s, 1)).astype(top_idx.dtype)
    sel_idx = jnp.concatenate([top_idx, own], axis=-1)
    sel_valid = jnp.concatenate([jnp.arange(kk)[None, :] < jnp.minimum(MOBA_TOPK, qblk)[:, None],
                                 jnp.ones((s, 1), dtype=bool)], axis=-1)
    n_sel = kk + 1
    nc = s // MOBA_QCHUNK
    q_c = q.reshape(b, h, nc, MOBA_QCHUNK, d).transpose(2, 0, 1, 3, 4)
    idx_c = sel_idx.reshape(b, h, nc, MOBA_QCHUNK, n_sel).transpose(2, 0, 1, 3, 4)
    valid_c = sel_valid.reshape(nc, MOBA_QCHUNK, n_sel)
    pos_c = qpos.reshape(nc, MOBA_QCHUNK)
    bi = jnp.arange(b)[:, None, None, None]
    hi = jnp.arange(h)[None, :, None, None]
    offs = jnp.arange(MOBA_BLOCK)

    def chunk(args):
        qc, idx, valid, pos = args
        ks = kb[bi, hi, idx]
        vs = vb[bi, hi, idx]
        kpos = idx[..., None] * MOBA_BLOCK + offs
        mask = valid[None, None, :, :, None] & (kpos <= pos[None, None, :, None, None])
        sc = jnp.einsum('bhqd,bhqnkd->bhqnk', qc, ks).astype(jnp.float32) * ATTN_SCALE
        sc = jnp.where(mask, sc, -jnp.inf)
        p = jax.nn.softmax(sc.reshape(b, h, MOBA_QCHUNK, n_sel * MOBA_BLOCK), axis=-1)
        p = p.reshape(b, h, MOBA_QCHUNK, n_sel, MOBA_BLOCK)
        return jnp.einsum('bhqnk,bhqnkd->bhqd', p, vs.astype(jnp.float32))

    o = lax.map(chunk, (q_c, idx_c, valid_c, pos_c))
    return o.transpose(1, 2, 0, 3, 4).reshape(b, h, s, d).astype(q.dtype)


def band_attention_stats(q, k, v, window):
    b, h, g, L, d = q.shape
    nq = -(-L // BAND)
    lp = nq * BAND
    pad = ((0, 0),) * 3 + ((0, lp - L), (0, 0))
    qb = jnp.pad(q, pad).reshape(b, h, g, nq, BAND, d)
    kb = jnp.pad(k, pad).reshape(b, h, g, nq, BAND, d)
    vb = jnp.pad(v, pad).reshape(b, h, g, nq, BAND, d)
    shift = ((0, 0),) * 3 + ((1, 0), (0, 0), (0, 0))
    kband = jnp.concatenate([jnp.pad(kb, shift)[:, :, :, :nq], kb], axis=-2)
    vband = jnp.concatenate([jnp.pad(vb, shift)[:, :, :, :nq], vb], axis=-2)
    qi = jnp.arange(BAND)[:, None]
    ki = jnp.arange(2 * BAND)[None, :]
    dist = qi + BAND - ki
    first = (jnp.arange(nq) == 0)[:, None, None]
    mask = (dist >= 0) & (dist <= window) & ~(first & (ki < BAND))
    sc = jnp.einsum('bhgnqd,bhgnkd->bhgnqk', qb, kband).astype(jnp.float32) * ATTN_SCALE
    sc = jnp.where(mask, sc, -jnp.inf)
    m = jnp.max(sc, axis=-1, keepdims=True)
    p = jnp.exp(sc - m)
    l = jnp.sum(p, axis=-1)
    o = jnp.einsum('bhgnqk,bhgnkd->bhgnqd', p, vband.astype(jnp.float32))
    return (o.reshape(b, h, g, lp, d)[:, :, :, :L],
            m[..., 0].reshape(b, h, g, lp)[..., :L],
            l.reshape(b, h, g, lp)[..., :L])


def dilated_attention(q, k, v):
    b, h, s, d = q.shape
    outs, maxes, dens = [], [], []
    for window, dil in DILATED_CFGS:
        L = s // dil

        def to_res(t, L=L, dil=dil):
            return t.reshape(b, h, L, dil, d).transpose(0, 1, 3, 2, 4)

        o, m, l = band_attention_stats(to_res(q), to_res(k), to_res(v), window // dil)
        outs.append(o.transpose(0, 1, 3, 2, 4).reshape(b, h, s, d))
        maxes.append(m.transpose(0, 1, 3, 2).reshape(b, h, s))
        dens.append(l.transpose(0, 1, 3, 2).reshape(b, h, s))
    mx = jnp.stack(maxes)
    wts = jnp.exp(mx - jnp.max(mx, axis=0, keepdims=True))
    num = jnp.sum(wts[..., None] * jnp.stack(outs), axis=0)
    den = jnp.sum(wts * jnp.stack(dens), axis=0)
    return (num / den[..., None]).astype(q.dtype)


def setup_inputs(seed: int = 0) -> dict:
    key = jax.random.key(seed)
    ks = jax.random.split(key, 24)
    f32 = jnp.float32

    def nrm(k, shape, scale):
        return jax.random.normal(k, shape, f32) * scale

    def gain(k, shape):
        return 1.0 + 0.02 * jax.random.normal(k, shape, f32)

    return {
        'x': nrm(ks[0], (BATCH, SEQ, D_MODEL), 1.0),
        'mem': nrm(ks[1], (BATCH, N_MEM, D_MODEL), 1.0),
        'norm_mix': gain(ks[2], (DEPTH, D_MODEL)),
        'w_in': nrm(ks[3], (DEPTH, D_MODEL, IN_COLS), D_MODEL ** -0.5),
        'q_norm_a': gain(ks[4], (DEPTH, HEAD_DIM)),
        'k_norm_a': gain(ks[5], (DEPTH, HEAD_DIM)),
        'q_norm_b': gain(ks[6], (DEPTH, HEAD_DIM)),
        'k_norm_b': gain(ks[7], (DEPTH, HEAD_DIM)),
        'q_norm_m': gain(ks[8], (DEPTH, HEAD_DIM)),
        'k_norm_m': gain(ks[9], (DEPTH, HEAD_DIM)),
        'mem_norm': gain(ks[10], (DEPTH, D_MODEL)),
        'w_mem_kv': nrm(ks[11], (DEPTH, D_MODEL, 2 * GROUP_WIDTH), D_MODEL ** -0.5),
        'conv_w': nrm(ks[12], (DEPTH, CONV_WIDTH, GROUP_WIDTH), CONV_WIDTH ** -0.5),
        'conv_b': nrm(ks[13], (DEPTH, GROUP_WIDTH), 0.02),
        'conv_ln_g': gain(ks[14], (DEPTH, GROUP_WIDTH)),
        'conv_ln_b': nrm(ks[15], (DEPTH, GROUP_WIDTH), 0.02),
        'w_conv_out': nrm(ks[16], (DEPTH, GROUP_WIDTH, GROUP_WIDTH), GROUP_WIDTH ** -0.5),
        'out_norm': gain(ks[17], (DEPTH, MIX_WIDTH)),
        'w_out': nrm(ks[18], (DEPTH, MIX_WIDTH, D_MODEL), MIX_WIDTH ** -0.5),
        'norm_ffn': gain(ks[19], (DEPTH, D_MODEL)),
        'w_up': nrm(ks[20], (DEPTH, D_MODEL, 2 * D_FF), D_MODEL ** -0.5),
        'ffn_conv_w': nrm(ks[21], (DEPTH, FFN_CONV_WIDTH, 2 * D_FF), FFN_CONV_WIDTH ** -0.5),
        'ffn_conv_b': nrm(ks[22], (DEPTH, 2 * D_FF), 0.02),
        'w_down': nrm(ks[23], (DEPTH, D_FF, D_MODEL), D_FF ** -0.5),
    }


def reference(x, mem, norm_mix, w_in, q_norm_a, k_norm_a, q_norm_b, k_norm_b, q_norm_m, k_norm_m,
              mem_norm, w_mem_kv, conv_w, conv_b, conv_ln_g, conv_ln_b, w_conv_out, out_norm, w_out,
              norm_ffn, w_up, ffn_conv_w, ffn_conv_b, w_down):
    b, s, _ = x.shape
    cos, sin = rope_tables(s)
    splits = [GROUP_WIDTH * j for j in range(1, 9)]
    for i in range(DEPTH):
        h = rms_norm(x, norm_mix[i])
        proj = jnp.einsum('bsd,de->bse', h, w_in[i])
        qa, ka, va, qb, kb, vb, c_val, c_gate, qm = jnp.split(proj, splits, axis=-1)

        qa = apply_rope(rms_norm(split_heads(qa, GROUP_HEADS), q_norm_a[i]), cos, sin)
        ka = apply_rope(rms_norm(split_heads(ka, GROUP_HEADS), k_norm_a[i]), cos, sin)
        o_a = merge_heads(moba_attention(qa, ka, split_heads(va, GROUP_HEADS)))

        qb = apply_rope(rms_norm(split_heads(qb, GROUP_HEADS), q_norm_b[i]), cos, sin)
        kb = apply_rope(rms_norm(split_heads(kb, GROUP_HEADS), k_norm_b[i]), cos, sin)
        o_b = merge_heads(dilated_attention(qb, kb, split_heads(vb, GROUP_HEADS)))

        u = c_val * jax.nn.sigmoid(c_gate)
        u = causal_depthwise_conv(u, conv_w[i], conv_b[i])
        u = jax.nn.silu(layer_norm(u, conv_ln_g[i], conv_ln_b[i]))
        o_c = jnp.einsum('bsc,ce->bse', u, w_conv_out[i])

        mh = rms_norm(mem, mem_norm[i])
        km, vm = jnp.split(jnp.einsum('bmd,de->bme', mh, w_mem_kv[i]), 2, axis=-1)
        qm = rms_norm(split_heads(qm, GROUP_HEADS), q_norm_m[i])
        km = rms_norm(split_heads(km, GROUP_HEADS), k_norm_m[i])
        sc = jnp.einsum('bhsd,bhmd->bhsm', qm, km).astype(jnp.float32) * ATTN_SCALE
        o_m = jnp.einsum('bhsm,bhmd->bhsd', jax.nn.softmax(sc, axis=-1),
                         split_heads(vm, GROUP_HEADS).astype(jnp.float32)).astype(x.dtype)
        o_m = merge_heads(o_m)

        mix = jnp.concatenate([o_a, o_b, o_c, o_m], axis=-1).reshape(b, s, N_GROUPS, GROUP_WIDTH)
        mix = rms_norm(mix, out_norm[i].reshape(N_GROUPS, GROUP_WIDTH)).reshape(b, s, MIX_WIDTH)
        x = x + jnp.einsum('bse,ed->bsd', mix, w_out[i])

        h = rms_norm(x, norm_ffn[i])
        u = causal_depthwise_conv(jnp.einsum('bsd,df->bsf', h, w_up[i]), ffn_conv_w[i], ffn_conv_b[i])
        gate, val = jnp.split(u, 2, axis=-1)
        x = x + jnp.einsum('bsf,fd->bsd', jax.nn.silu(gate) * val, w_down[i])
    return x
```

```python
import functools

import jax
import jax.numpy as jnp
from jax import lax
from jax.experimental import pallas as pl
from jax.experimental.pallas import tpu as pltpu

F32 = jnp.float32
BF16 = jnp.bfloat16

D_MODEL = 1024
HEAD_DIM = 64
GROUP_WIDTH = 256
GROUP_HEADS = 4
N_PROJ_GROUPS = 9
MOBA_BLOCK = 256
MOBA_TOPK = 3
BAND = 128
DIL_MID = 4
DIL_MAX = 16
CONV_WIDTH = 31
FFN_CONV_WIDTH = 3
D_FF = 2816
ROPE_THETA = 10000.0
EPS = 1e-6
ATTN_SCALE = HEAD_DIM ** -0.5
NEG = -1e30

ROW_TILE = 512
FFN_CHUNK = 256
HALO = 16
VMEM_LIMIT = 56 * 1024 * 1024


def _params(sem):
    return pltpu.CompilerParams(dimension_semantics=sem, vmem_limit_bytes=VMEM_LIMIT)


def _lane_head(shape):
    return lax.broadcasted_iota(jnp.int32, shape, 1) // HEAD_DIM


def _head_ones():
    r = lax.broadcasted_iota(jnp.int32, (GROUP_WIDTH, GROUP_WIDTH), 0) // HEAD_DIM
    c = lax.broadcasted_iota(jnp.int32, (GROUP_WIDTH, GROUP_WIDTH), 1) // HEAD_DIM
    return jnp.where(r == c, 1.0, 0.0).astype(BF16)


def _head_rms(p, gain, ones_bd):
    ss = jnp.dot((p * p).astype(BF16), ones_bd, preferred_element_type=F32)
    return p * lax.rsqrt(ss * (1.0 / HEAD_DIM) + EPS) * gain


def _row_rms(x, gain):
    return x * lax.rsqrt(jnp.mean(x * x, axis=-1, keepdims=True) + EPS) * gain


def _dot_nt(a, b):
    return lax.dot_general(a, b, (((1,), (1,)), ((), ())), preferred_element_type=F32)


def _mask_heads(q):
    lh = _lane_head(q.shape)
    return [jnp.where(lh == h, q, jnp.zeros_like(q)) for h in range(GROUP_HEADS)]


def _attn_tile(qms, k, v, biases):
    tq = qms[0].shape[0]
    lh = _lane_head((tq, GROUP_WIDTH))
    acc = m_all = l_all = None
    for h in range(GROUP_HEADS):
        s = _dot_nt(qms[h], k)
        if biases is not None and biases[h] is not None:
            s = s + biases[h]
        m = jnp.max(s, axis=1, keepdims=True)
        p = jnp.exp(s - m)
        l = jnp.sum(p, axis=1, keepdims=True)
        o = jnp.dot(p.astype(BF16), v, preferred_element_type=F32)
        mb = jnp.broadcast_to(m, (tq, GROUP_WIDTH))
        lb = jnp.broadcast_to(l, (tq, GROUP_WIDTH))
        if h == 0:
            acc, m_all, l_all = o, mb, lb
        else:
            sel = lh == h
            acc = jnp.where(sel, o, acc)
            m_all = jnp.where(sel, mb, m_all)
            l_all = jnp.where(sel, lb, l_all)
    return acc, m_all, l_all


def _merge(acc0, m0, l0, acc1, m1, l1):
    mn = jnp.maximum(m0, m1)
    a0 = jnp.exp(m0 - mn)
    a1 = jnp.exp(m1 - mn)
    return acc0 * a0 + acc1 * a1, mn, l0 * a0 + l1 * a1


def _in_proj_kernel(x_ref, g_ref, w_ref, cos_ref, sin_ref, gains_ref,
                    qa_ref, ka_ref, va_ref, qb_ref, kb_ref, vb_ref, u_ref, qm_ref):
    h = _row_rms(x_ref[...], g_ref[...]).astype(BF16)
    ones_bd = _head_ones()
    cos = cos_ref[...]
    sin = sin_ref[...]
    tm = h.shape[0]
    lower = (lax.broadcasted_iota(jnp.int32, (tm, GROUP_WIDTH), 1) % HEAD_DIM) < (HEAD_DIM // 2)

    def proj(j):
        return jnp.dot(h, w_ref[:, j * GROUP_WIDTH:(j + 1) * GROUP_WIDTH], preferred_element_type=F32)

    def rope(y):
        rot = jnp.where(lower, pltpu.roll(y, GROUP_WIDTH - HEAD_DIM // 2, 1), pltpu.roll(y, HEAD_DIM // 2, 1))
        return y * cos + rot * sin

    def qk(j, gain_row, scale):
        y = rope(_head_rms(proj(j), gains_ref[gain_row:gain_row + 1, :], ones_bd))
        return (y * scale).astype(BF16) if scale != 1.0 else y.astype(BF16)

    qa_ref[...] = qk(0, 0, ATTN_SCALE)
    ka_ref[...] = qk(1, 1, 1.0)
    va_ref[...] = proj(2).astype(BF16)
    qb_ref[...] = qk(3, 2, ATTN_SCALE)
    kb_ref[...] = qk(4, 3, 1.0)
    vb_ref[...] = proj(5).astype(BF16)
    c_val = proj(6)
    c_gate = proj(7)
    u_ref[...] = (c_val * (1.0 / (1.0 + jnp.exp(-c_gate)))).astype(BF16)
    qm = _head_rms(proj(8), gains_ref[4:5, :], ones_bd)
    qm_ref[...] = (qm * ATTN_SCALE).astype(BF16)


def _in_proj(x2, g, w, cos, sin, gains, seq):
    rows = x2.shape[0]
    tm = ROW_TILE
    pos_blocks = seq // tm
    out = jax.ShapeDtypeStruct((rows, GROUP_WIDTH), BF16)
    ospec = pl.BlockSpec((tm, GROUP_WIDTH), lambda i: (i, 0))
    return pl.pallas_call(
        _in_proj_kernel,
        grid=(rows // tm,),
        in_specs=[
            pl.BlockSpec((tm, D_MODEL), lambda i: (i, 0)),
            pl.BlockSpec((1, D_MODEL), lambda i: (0, 0)),
            pl.BlockSpec((D_MODEL, N_PROJ_GROUPS * GROUP_WIDTH), lambda i: (0, 0)),
            pl.BlockSpec((tm, GROUP_WIDTH), lambda i: (i % pos_blocks, 0)),
            pl.BlockSpec((tm, GROUP_WIDTH), lambda i: (i % pos_blocks, 0)),
            pl.BlockSpec((8, GROUP_WIDTH), lambda i: (0, 0)),
        ],
        out_specs=[ospec] * 8,
        out_shape=[out] * 8,
        compiler_params=_params(("parallel",)),
        name="in_proj",
    )(x2, g, w, cos, sin, gains)


def _moba_kernel(q_ref, k_ref, v_ref, gout_ref, o_ref, khi_ref, klo_ref, selb_ref, acc_ref, m_ref, l_ref):
    i = pl.program_id(1)
    seq = k_ref.shape[1]
    nb = seq // MOBA_BLOCK
    tq = MOBA_BLOCK

    @pl.when(i == 0)
    def _():
        r = lax.broadcasted_iota(jnp.int32, (BAND, seq), 0)
        c = lax.broadcasted_iota(jnp.int32, (BAND, seq), 1) // MOBA_BLOCK
        avg = jnp.where(r == c, 1.0 / MOBA_BLOCK, 0.0).astype(BF16)
        kmean = jnp.dot(avg, k_ref[0], preferred_element_type=F32)
        hi = kmean.astype(BF16)
        khi_ref[...] = hi
        klo_ref[...] = (kmean - hi.astype(F32)).astype(BF16)

    row0 = pl.multiple_of(i * tq, tq)
    qms = _mask_heads(q_ref[0, pl.ds(row0, tq), :])

    lane_n = lax.broadcasted_iota(jnp.int32, (tq, BAND), 1)
    past = lane_n < i
    for h in range(GROUP_HEADS):
        g = _dot_nt(qms[h], khi_ref[...]) + _dot_nt(qms[h], klo_ref[...])
        g = jnp.where(past, g, -jnp.inf)
        rank = jnp.zeros((tq, BAND), F32)
        for mth in range(nb - 1):
            col = g[:, mth:mth + 1]
            beats = jnp.where(col == g, jnp.where(lane_n > mth, 1.0, 0.0), jnp.where(col > g, 1.0, 0.0))
            rank = rank + beats
        selb_ref[h] = jnp.where(past, jnp.where(rank < MOBA_TOPK, 0.0, NEG), NEG)

    rr = lax.broadcasted_iota(jnp.int32, (tq, tq), 0)
    cc = lax.broadcasted_iota(jnp.int32, (tq, tq), 1)
    causal = jnp.where(rr >= cc, 0.0, NEG)
    acc, m_all, l_all = _attn_tile(qms, k_ref[0, pl.ds(row0, tq), :], v_ref[0, pl.ds(row0, tq), :],
                                   [causal] * GROUP_HEADS)
    acc_ref[...] = acc
    m_ref[...] = m_all
    l_ref[...] = l_all

    for n in range(nb - 1):
        @pl.when(n < i)
        def _(n=n):
            biases = [selb_ref[h, :, n:n + 1] for h in range(GROUP_HEADS)]
            a1, m1, l1 = _attn_tile(qms, k_ref[0, n * tq:(n + 1) * tq, :], v_ref[0, n * tq:(n + 1) * tq, :],
                                    biases)
            a, m, l = _merge(acc_ref[...], m_ref[...], l_ref[...], a1, m1, l1)
            acc_ref[...] = a
            m_ref[...] = m
            l_ref[...] = l

    o_ref[0] = _row_rms(acc_ref[...] / l_ref[...], gout_ref[...]).astype(BF16)


def _moba(q, k, v, gout):
    b, seq, _ = q.shape
    full = pl.BlockSpec((1, seq, GROUP_WIDTH), lambda bi, i: (bi, 0, 0))
    return pl.pallas_call(
        _moba_kernel,
        grid=(b, seq // MOBA_BLOCK),
        in_specs=[full, full, full, pl.BlockSpec((1, GROUP_WIDTH), lambda bi, i: (0, 0))],
        out_specs=pl.BlockSpec((1, MOBA_BLOCK, GROUP_WIDTH), lambda bi, i: (bi, i, 0)),
        out_shape=jax.ShapeDtypeStruct((b, seq, GROUP_WIDTH), BF16),
        scratch_shapes=[
            pltpu.VMEM((BAND, GROUP_WIDTH), BF16),
            pltpu.VMEM((BAND, GROUP_WIDTH), BF16),
            pltpu.VMEM((GROUP_HEADS, MOBA_BLOCK, BAND), F32),
            pltpu.VMEM((MOBA_BLOCK, GROUP_WIDTH), F32),
            pltpu.VMEM((MOBA_BLOCK, GROUP_WIDTH), F32),
            pltpu.VMEM((MOBA_BLOCK, GROUP_WIDTH), F32),
        ],
        compiler_params=_params(("parallel", "arbitrary")),
        name="moba",
    )(q, k, v, gout)


def _dilated_kernel(q16_ref, k1_ref, k4_ref, k16_ref, v1_ref, v4_ref, v16_ref,
                    b1_ref, b4_ref, b16_ref, gout_ref, o_ref, acc_ref, m_ref, l_ref):
    gw = GROUP_WIDTH
    rows16 = q16_ref.shape[1]
    seq = rows16 * DIL_MAX

    def cls(r):
        return slice(r * gw, (r + 1) * gw)

    for r in range(DIL_MAX):
        qms = _mask_heads(q16_ref[0, :, cls(r)])
        acc, m_all, l_all = _attn_tile(qms, k16_ref[0, :, cls(r)], v16_ref[0, :, cls(r)],
                                       [b16_ref[...]] * GROUP_HEADS)
        acc_ref[:, cls(r)] = acc
        m_ref[:, cls(r)] = m_all
        l_ref[:, cls(r)] = l_all

    per = DIL_MAX // DIL_MID
    jrows = BAND // per
    for a in range(DIL_MID):
        for jb in range(seq // DIL_MID // BAND):
            rs = slice(jb * jrows, (jb + 1) * jrows)
            q = jnp.concatenate([q16_ref[0, rs, cls(per_u * DIL_MID + a)] for per_u in range(per)], axis=0)
            qms = _mask_heads(q)
            if jb == 0:
                ks = slice(0, BAND)
                bias = b4_ref[:, BAND:]
            else:
                ks = slice((jb - 1) * BAND, (jb + 1) * BAND)
                bias = b4_ref[...]
            a1, m1, l1 = _attn_tile(qms, k4_ref[0, ks, cls(a)], v4_ref[0, ks, cls(a)], [bias] * GROUP_HEADS)
            for u in range(per):
                cs = cls(u * DIL_MID + a)
                ts = slice(u * jrows, (u + 1) * jrows)
                acc, m_all, l_all = _merge(acc_ref[rs, cs], m_ref[rs, cs], l_ref[rs, cs],
                                           a1[ts], m1[ts], l1[ts])
                acc_ref[rs, cs] = acc
                m_ref[rs, cs] = m_all
                l_ref[rs, cs] = l_all

    tq = 2 * BAND
    jrows = tq // DIL_MAX
    gout = gout_ref[...]
    for jb in range(seq // tq):
        rs = slice(jb * jrows, (jb + 1) * jrows)
        q = jnp.concatenate([q16_ref[0, rs, cls(r)] for r in range(DIL_MAX)], axis=0)
        qms = _mask_heads(q)
        if jb == 0:
            ks = slice(0, tq)
            bias = b1_ref[:, BAND:]
        else:
            ks = slice(jb * tq - BAND, (jb + 1) * tq)
            bias = b1_ref[...]
        a1, m1, l1 = _attn_tile(qms, k1_ref[0, ks, :], v1_ref[0, ks, :], [bias] * GROUP_HEADS)
        for r in range(DIL_MAX):
            ts = slice(r * jrows, (r + 1) * jrows)
            acc, _, l_all = _merge(acc_ref[rs, cls(r)], m_ref[rs, cls(r)], l_ref[rs, cls(r)],
                                   a1[ts], m1[ts], l1[ts])
            o_ref[0, rs, cls(r)] = _row_rms(acc / l_all, gout).astype(BF16)


def _dilated_biases():
    def band(qrel, nk):
        dist = qrel[:, None] + BAND - jnp.arange(nk)[None, :]
        return jnp.where((dist >= 0) & (dist <= BAND), 0.0, NEG).astype(F32)

    tq = 2 * BAND
    row = jnp.arange(tq)
    b1 = band(DIL_MAX * (row % (tq // DIL_MAX)) + row // (tq // DIL_MAX), BAND + tq)
    row = jnp.arange(BAND)
    per = DIL_MAX // DIL_MID
    b4 = band(per * (row % (BAND // per)) + row // (BAND // per), 2 * BAND)
    b16 = band(row, 2 * BAND)[:, BAND:]
    return b1, b4, b16


def _dilated(q, k, v, gout):
    b, seq, gw = q.shape
    b1, b4, b16 = _dilated_biases()
    v16 = lambda t: t.reshape(b, seq // DIL_MAX, DIL_MAX * gw)
    v4 = lambda t: t.reshape(b, seq // DIL_MID, DIL_MID * gw)
    batch = lambda shape: pl.BlockSpec((1,) + shape, lambda bi: (bi, 0, 0))
    const = lambda shape: pl.BlockSpec(shape, lambda bi: (0, 0))
    s16 = (seq // DIL_MAX, DIL_MAX * gw)
    s4 = (seq // DIL_MID, DIL_MID * gw)
    s1 = (seq, gw)
    out = pl.pallas_call(
        _dilated_kernel,
        grid=(b,),
        in_specs=[batch(s16), batch(s1), batch(s4), batch(s16), batch(s1), batch(s4), batch(s16),
                  const(b1.shape), const(b4.shape), const(b16.shape), const((1, gw))],
        out_specs=batch(s16),
        out_shape=jax.ShapeDtypeStruct((b,) + s16, BF16),
        scratch_shapes=[pltpu.VMEM(s16, F32)] * 3,
        compiler_params=_params(("parallel",)),
        name="dilated",
    )(v16(q), k, v4(k), v16(k), v, v4(v), v16(v), b1, b4, b16, gout)
    return out.reshape(b, seq, gw)


CONV_PAD = 32
CONV_ROWS = 256


def _conv_kernel(u_ref, w_ref, b_ref, lng_ref, lnb_ref, wout_ref, gout_ref, o_ref, pad_ref):
    seq = u_ref.shape[1]
    pad_ref[0:CONV_PAD, :] = jnp.zeros((CONV_PAD, GROUP_WIDTH), F32)
    pad_ref[CONV_PAD:CONV_PAD + seq, :] = u_ref[0].astype(F32)
    pad_ref[CONV_PAD + seq:, :] = jnp.zeros((16, GROUP_WIDTH), F32)
    lead = CONV_PAD - (CONV_WIDTH - 1)
    span = CONV_ROWS + 16
    for c0 in range(0, seq, CONV_ROWS):
        y = jnp.broadcast_to(b_ref[...], (CONV_ROWS, GROUP_WIDTH))
        for sub in range(8):
            z = None
            for a in range((CONV_WIDTH - sub + 7) // 8):
                j = 8 * a + sub
                t = w_ref[j:j + 1, :] * pad_ref[c0 + 8 * a:c0 + 8 * a + span, :]
                z = t if z is None else z + t
            y = y + z[lead + sub:lead + sub + CONV_ROWS]
        mu = jnp.mean(y, axis=-1, keepdims=True)
        d = y - mu
        var = jnp.mean(d * d, axis=-1, keepdims=True)
        yn = d * lax.rsqrt(var + EPS) * lng_ref[...] + lnb_ref[...]
        act = yn * (1.0 / (1.0 + jnp.exp(-yn)))
        oc = jnp.dot(act.astype(BF16), wout_ref[...], preferred_element_type=F32)
        o_ref[0, c0:c0 + CONV_ROWS, :] = _row_rms(oc, gout_ref[...]).astype(BF16)


def _conv_module(u, w, bias, lng, lnb, wout, gout):
    b, seq, gw = u.shape
    const = lambda shape: pl.BlockSpec(shape, lambda bi: (0, 0))
    return pl.pallas_call(
        _conv_kernel,
        grid=(b,),
        in_specs=[pl.BlockSpec((1, seq, gw), lambda bi: (bi, 0, 0)),
                  const((CONV_PAD, gw)), const((1, gw)), const((1, gw)), const((1, gw)),
                  const((gw, gw)), const((1, gw))],
        out_specs=pl.BlockSpec((1, seq, gw), lambda bi: (bi, 0, 0)),
        out_shape=jax.ShapeDtypeStruct((b, seq, gw), BF16),
        scratch_shapes=[pltpu.VMEM((CONV_PAD + seq + 16, gw), F32)],
        compiler_params=_params(("parallel",)),
        name="conv_module",
    )(u, w, bias, lng, lnb, wout, gout)


MEM_QROWS = 512


def _mem_kernel(q_ref, mem_ref, gmem_ref, wkv_ref, gk_ref, gout_ref, o_ref):
    seq = q_ref.shape[1]
    mh = _row_rms(mem_ref[0], gmem_ref[...]).astype(BF16)
    kv = jnp.dot(mh, wkv_ref[...], preferred_element_type=F32)
    km = _head_rms(kv[:, :GROUP_WIDTH], gk_ref[...], _head_ones()).astype(BF16)
    vm = kv[:, GROUP_WIDTH:].astype(BF16)
    for c0 in range(0, seq, MEM_QROWS):
        qms = _mask_heads(q_ref[0, c0:c0 + MEM_QROWS, :])
        acc, _, l_all = _attn_tile(qms, km, vm, None)
        o_ref[0, c0:c0 + MEM_QROWS, :] = _row_rms(acc / l_all, gout_ref[...]).astype(BF16)


def _mem_attn(q, mem, gmem, wkv, gk, gout):
    b, seq, gw = q.shape
    n_mem = mem.shape[1]
    const = lambda shape: pl.BlockSpec(shape, lambda bi: (0, 0))
    return pl.pallas_call(
        _mem_kernel,
        grid=(b,),
        in_specs=[pl.BlockSpec((1, seq, gw), lambda bi: (bi, 0, 0)),
                  pl.BlockSpec((1, n_mem, D_MODEL), lambda bi: (bi, 0, 0)),
                  const((1, D_MODEL)), const((D_MODEL, 2 * gw)), const((1, gw)), const((1, gw))],
        out_specs=pl.BlockSpec((1, seq, gw), lambda bi: (bi, 0, 0)),
        out_shape=jax.ShapeDtypeStruct((b, seq, gw), BF16),
        compiler_params=_params(("parallel",)),
        name="mem_attn",
    )(q, mem, gmem, wkv, gk, gout)


def _out_proj_kernel(x_ref, oa_ref, ob_ref, oc_ref, om_ref, w_ref, o_ref):
    acc = x_ref[...]
    for g, ref in enumerate((oa_ref, ob_ref, oc_ref, om_ref)):
        acc = acc + jnp.dot(ref[...], w_ref[g * GROUP_WIDTH:(g + 1) * GROUP_WIDTH, :],
                            preferred_element_type=F32)
    o_ref[...] = acc


def _out_proj(x2, oa, ob, oc, om, w):
    rows = x2.shape[0]
    tm = ROW_TILE
    gspec = pl.BlockSpec((tm, GROUP_WIDTH), lambda i: (i, 0))
    xspec = pl.BlockSpec((tm, D_MODEL), lambda i: (i, 0))
    return pl.pallas_call(
        _out_proj_kernel,
        grid=(rows // tm,),
        in_specs=[xspec, gspec, gspec, gspec, gspec, pl.BlockSpec((D_MODEL, D_MODEL), lambda i: (0, 0))],
        out_specs=xspec,
        out_shape=jax.ShapeDtypeStruct((rows, D_MODEL), F32),
        compiler_params=_params(("parallel",)),
        name="out_proj",
    )(x2, oa, ob, oc, om, w)


def _ffn_kernel(pos_blocks, x_ref, xh_ref, g_ref, wup_ref, cw_ref, cb_ref, wdown_ref, o_ref, h_ref, act_ref):
    tm = x_ref.shape[0]
    x = x_ref[...]
    keep = jnp.where(pl.program_id(0) % pos_blocks == 0, 0.0, 1.0)
    h_ref[0:HALO, :] = (_row_rms(xh_ref[...], g_ref[...]) * keep).astype(BF16)
    h_ref[HALO:, :] = _row_rms(x, g_ref[...]).astype(BF16)
    h = h_ref[...]

    def conv(col0):
        cs = slice(col0, col0 + FFN_CHUNK)
        u = jnp.dot(h, wup_ref[:, cs], preferred_element_type=F32)
        y = cb_ref[:, cs] + cw_ref[2:3, cs] * u[HALO:HALO + tm]
        y = y + cw_ref[1:2, cs] * u[HALO - 1:HALO - 1 + tm]
        return y + cw_ref[0:1, cs] * u[HALO - 2:HALO - 2 + tm]

    for c in range(D_FF // FFN_CHUNK):
        gate = conv(c * FFN_CHUNK)
        val = conv(D_FF + c * FFN_CHUNK)
        act = gate * (1.0 / (1.0 + jnp.exp(-gate))) * val
        act_ref[:, c * FFN_CHUNK:(c + 1) * FFN_CHUNK] = act.astype(BF16)
    o_ref[...] = x + jnp.dot(act_ref[...], wdown_ref[...], preferred_element_type=F32)


def _ffn(x2, g, wup, cw, cb, wdown, seq):
    rows = x2.shape[0]
    tm = ROW_TILE
    pos_blocks = seq // tm
    hb = tm // HALO
    const = lambda shape: pl.BlockSpec(shape, lambda i: (0, 0))
    xspec = pl.BlockSpec((tm, D_MODEL), lambda i: (i, 0))
    return pl.pallas_call(
        functools.partial(_ffn_kernel, pos_blocks),
        grid=(rows // tm,),
        in_specs=[xspec,
                  pl.BlockSpec((HALO, D_MODEL), lambda i: (jnp.maximum(i * hb - 1, 0), 0)),
                  const((1, D_MODEL)), const((D_MODEL, 2 * D_FF)), const((8, 2 * D_FF)),
                  const((1, 2 * D_FF)), const((D_FF, D_MODEL))],
        out_specs=xspec,
        out_shape=jax.ShapeDtypeStruct((rows, D_MODEL), F32),
        scratch_shapes=[pltpu.VMEM((HALO + tm, D_MODEL), BF16), pltpu.VMEM((tm, D_FF), BF16)],
        compiler_params=_params(("parallel",)),
        name="ffn",
    )(x2, x2, g, wup, cw, cb, wdown)


def _rope_tables(seq):
    inv = ROPE_THETA ** (-jnp.arange(0, HEAD_DIM, 2, dtype=F32) / HEAD_DIM)
    ang = jnp.arange(seq, dtype=F32)[:, None] * inv[None, :]
    ang = jnp.concatenate([ang, ang], axis=-1)
    sign = jnp.where(jnp.arange(HEAD_DIM) < HEAD_DIM // 2, -1.0, 1.0).astype(F32)
    cos = jnp.tile(jnp.cos(ang), (1, GROUP_HEADS))
    sin = jnp.tile(jnp.sin(ang) * sign[None, :], (1, GROUP_HEADS))
    return cos, sin


def _pad_rows(a, rows):
    return jnp.concatenate([a, jnp.zeros((rows - a.shape[0],) + a.shape[1:], a.dtype)], axis=0)


def kernel(x, mem, norm_mix, w_in, q_norm_a, k_norm_a, q_norm_b, k_norm_b, q_norm_m, k_norm_m,
           mem_norm, w_mem_kv, conv_w, conv_b, conv_ln_g, conv_ln_b, w_conv_out, out_norm, w_out,
           norm_ffn, w_up, ffn_conv_w, ffn_conv_b, w_down):
    b, seq, d = x.shape
    depth = w_in.shape[0]
    cos, sin = _rope_tables(seq)
    x2 = x.reshape(b * seq, d)
    tile_h = lambda g: jnp.tile(g, GROUP_HEADS)
    for i in range(depth):
        gains = _pad_rows(jnp.stack([tile_h(q_norm_a[i]), tile_h(k_norm_a[i]), tile_h(q_norm_b[i]),
                                     tile_h(k_norm_b[i]), tile_h(q_norm_m[i])]), 8)
        qa, ka, va, qb, kb, vb, u, qm = _in_proj(x2, norm_mix[i][None], w_in[i].astype(BF16), cos, sin, gains, seq)
        r3 = lambda t: t.reshape(b, seq, GROUP_WIDTH)
        gout = out_norm[i].reshape(4, 1, GROUP_WIDTH)
        oa = _moba(r3(qa), r3(ka), r3(va), gout[0])
        ob = _dilated(r3(qb), r3(kb), r3(vb), gout[1])
        oc = _conv_module(r3(u), _pad_rows(conv_w[i], CONV_PAD), conv_b[i][None], conv_ln_g[i][None],
                          conv_ln_b[i][None], w_conv_out[i].astype(BF16), gout[2])
        om = _mem_attn(r3(qm), mem, mem_norm[i][None], w_mem_kv[i].astype(BF16),
                       tile_h(k_norm_m[i])[None], gout[3])
        f2 = lambda t: t.reshape(b * seq, GROUP_WIDTH)
        x2 = _out_proj(x2, f2(oa), f2(ob), f2(oc), f2(om), w_out[i].astype(BF16))
        x2 = _ffn(x2, norm_ffn[i][None], w_up[i].astype(BF16), _pad_rows(ffn_conv_w[i], 8),
                  ffn_conv_b[i][None], w_down[i].astype(BF16), seq)
    return x2.reshape(b, seq, d)
```

```python
import functools

import jax
import jax.numpy as jnp
from jax import lax
from jax.experimental import pallas as pl
from jax.experimental.pallas import tpu as pltpu

F32 = jnp.float32
BF16 = jnp.bfloat16

D_MODEL = 1024
HEAD_DIM = 64
GROUP_WIDTH = 256
GROUP_HEADS = 4
N_PROJ_GROUPS = 9
MOBA_BLOCK = 256
MOBA_TOPK = 3
BAND = 128
DIL_MID = 4
DIL_MAX = 16
CONV_WIDTH = 31
FFN_CONV_WIDTH = 3
D_FF = 2816
ROPE_THETA = 10000.0
EPS = 1e-6
ATTN_SCALE = HEAD_DIM ** -0.5
Q_SCALE = ATTN_SCALE * 1.4426950408889634
NEG = -1e30

ROW_TILE = 512
FFN_CHUNK = 256
HALO = 16
STAT_ROWS = 8
VMEM_LIMIT = 56 * 1024 * 1024


def _params(sem):
    return pltpu.CompilerParams(dimension_semantics=sem, vmem_limit_bytes=VMEM_LIMIT)


def _lane_head(shape):
    return lax.broadcasted_iota(jnp.int32, shape, 1) // HEAD_DIM


def _head_ones():
    r = lax.broadcasted_iota(jnp.int32, (GROUP_WIDTH, GROUP_WIDTH), 0) // HEAD_DIM
    c = lax.broadcasted_iota(jnp.int32, (GROUP_WIDTH, GROUP_WIDTH), 1) // HEAD_DIM
    return jnp.where(r == c, 1.0, 0.0).astype(BF16)


def _head_rms(p, gain, ones_bd):
    ss = jnp.dot((p * p).astype(BF16), ones_bd, preferred_element_type=F32)
    return p * lax.rsqrt(ss * (1.0 / HEAD_DIM) + EPS) * gain


def _row_rms(x, gain):
    return x * lax.rsqrt(jnp.mean(x * x, axis=-1, keepdims=True) + EPS) * gain


def _dot_nt(a, b):
    return lax.dot_general(a, b, (((1,), (1,)), ((), ())), preferred_element_type=F32)


def _mask_heads(q):
    lh = _lane_head((1, GROUP_WIDTH))
    return [q * jnp.where(lh == h, 1.0, 0.0).astype(BF16) for h in range(GROUP_HEADS)]


def _head_slice(h):
    return slice(h * HEAD_DIM, (h + 1) * HEAD_DIM)


def _transpose_bf16(v):
    return v.astype(F32).T.astype(BF16)


class _Pipe:
    def __init__(self, depth=3):
        self.depth = depth
        self.items = []

    def push(self, score, finish):
        self.items.append((score(), finish))
        if len(self.items) > self.depth:
            s, f = self.items.pop(0)
            f(s)

    def flush(self):
        for s, f in self.items:
            f(s)
        self.items = []


def _softmax_pv(s, vt_h):
    m = jnp.max(s, axis=0, keepdims=True)
    p = jnp.exp2(s - m)
    l = jnp.sum(p, axis=0, keepdims=True)
    return jnp.dot(vt_h, p.astype(BF16), preferred_element_type=F32), m, l


def _head_rows(parts):
    tq = parts[0][0].shape[1]
    rows = lambda xs: jnp.concatenate(xs, axis=0).T
    return (rows([o for o, _, _ in parts]),
            rows([jnp.broadcast_to(m, (HEAD_DIM, tq)) for _, m, _ in parts]),
            rows([jnp.broadcast_to(l, (HEAD_DIM, tq)) for _, _, l in parts]))


def _merge(acc0, m0, l0, acc1, m1, l1):
    mn = jnp.maximum(m0, m1)
    a0 = jnp.exp2(m0 - mn)
    a1 = jnp.exp2(m1 - mn)
    return acc0 * a0 + acc1 * a1, mn, l0 * a0 + l1 * a1


def _in_proj_kernel(x_ref, g_ref, w_ref, wvat_ref, cos_ref, sin_ref, gains_ref,
                    qa_ref, ka_ref, vat_ref, qbc_ref, kb_ref, kbc_ref, vb_ref, vbc_ref, u_ref, qm_ref,
                    perm_ref):
    h = _row_rms(x_ref[...], g_ref[...]).astype(BF16)
    ones_bd = _head_ones()
    cos = cos_ref[...]
    sin = sin_ref[...]
    tm = h.shape[0]
    lower = (lax.broadcasted_iota(jnp.int32, (tm, GROUP_WIDTH), 1) % HEAD_DIM) < (HEAD_DIM // 2)

    def proj(j):
        return jnp.dot(h, w_ref[:, j * GROUP_WIDTH:(j + 1) * GROUP_WIDTH], preferred_element_type=F32)

    def rope(y):
        rot = jnp.where(lower, pltpu.roll(y, GROUP_WIDTH - HEAD_DIM // 2, 1), pltpu.roll(y, HEAD_DIM // 2, 1))
        return y * cos + rot * sin

    def qk(j, gain_row):
        return rope(_head_rms(proj(j), gains_ref[gain_row:gain_row + 1, :], ones_bd))

    def class_major(y, out_ref):
        for half in range(GROUP_WIDTH // BAND):
            ls = slice(half * BAND, (half + 1) * BAND)
            perm_ref[half] = y[:, ls]
            for r in range(DIL_MAX):
                out_ref[0, r, :, ls] = perm_ref[half, pl.ds(r, tm // DIL_MAX, stride=DIL_MAX), :].astype(BF16)

    qa_ref[...] = (qk(0, 0) * Q_SCALE).astype(BF16)
    ka_ref[...] = qk(1, 1).astype(BF16)
    vat = _dot_nt(wvat_ref[...], h).astype(BF16)
    for j in range(tm // MOBA_BLOCK):
        vat_ref[0, j] = vat[:, j * MOBA_BLOCK:(j + 1) * MOBA_BLOCK]
    class_major(qk(3, 2) * Q_SCALE, qbc_ref)
    kb = qk(4, 3)
    kb_ref[...] = kb.astype(BF16)
    class_major(kb, kbc_ref)
    vb = proj(5)
    vb_ref[...] = vb.astype(BF16)
    class_major(vb, vbc_ref)
    c_val = proj(6)
    c_gate = proj(7)
    u_ref[...] = (c_val * (1.0 / (1.0 + jnp.exp(-c_gate)))).astype(BF16)
    qm = _head_rms(proj(8), gains_ref[4:5, :], ones_bd)
    qm_ref[...] = (qm * Q_SCALE).astype(BF16)


def _in_proj(x2, g, w, wvat, cos, sin, gains, seq):
    rows = x2.shape[0]
    b = rows // seq
    tm = ROW_TILE
    pos_blocks = seq // tm
    nat = jax.ShapeDtypeStruct((rows, GROUP_WIDTH), BF16)
    nat_spec = pl.BlockSpec((tm, GROUP_WIDTH), lambda i: (i, 0))
    cm = jax.ShapeDtypeStruct((b, DIL_MAX, seq // DIL_MAX, GROUP_WIDTH), BF16)
    cm_spec = pl.BlockSpec((1, DIL_MAX, tm // DIL_MAX, GROUP_WIDTH),
                           lambda i: (i // pos_blocks, 0, i % pos_blocks, 0))
    vat = jax.ShapeDtypeStruct((b, seq // MOBA_BLOCK, GROUP_WIDTH, MOBA_BLOCK), BF16)
    vat_spec = pl.BlockSpec((1, tm // MOBA_BLOCK, GROUP_WIDTH, MOBA_BLOCK),
                            lambda i: (i // pos_blocks, i % pos_blocks, 0, 0))
    return pl.pallas_call(
        _in_proj_kernel,
        grid=(rows // tm,),
        in_specs=[
            pl.BlockSpec((tm, D_MODEL), lambda i: (i, 0)),
            pl.BlockSpec((1, D_MODEL), lambda i: (0, 0)),
            pl.BlockSpec((D_MODEL, N_PROJ_GROUPS * GROUP_WIDTH), lambda i: (0, 0)),
            pl.BlockSpec((GROUP_WIDTH, D_MODEL), lambda i: (0, 0)),
            pl.BlockSpec((tm, GROUP_WIDTH), lambda i: (i % pos_blocks, 0)),
            pl.BlockSpec((tm, GROUP_WIDTH), lambda i: (i % pos_blocks, 0)),
            pl.BlockSpec((8, GROUP_WIDTH), lambda i: (0, 0)),
        ],
        out_specs=[nat_spec, nat_spec, vat_spec, cm_spec, nat_spec, cm_spec, nat_spec, cm_spec, nat_spec, nat_spec],
        out_shape=[nat, nat, vat, cm, nat, cm, nat, cm, nat, nat],
        scratch_shapes=[pltpu.VMEM((GROUP_WIDTH // BAND, tm, BAND), F32)],
        compiler_params=_params(("parallel",)),
        name="in_proj",
    )(x2, g, w, wvat, cos, sin, gains)


def _moba_kernel(q_ref, k_ref, vt_ref, gout_ref, o_ref):
    seq = k_ref.shape[1]
    nb = seq // MOBA_BLOCK
    tq = MOBA_BLOCK
    nrow = 2 * STAT_ROWS

    r = lax.broadcasted_iota(jnp.int32, (nrow, seq), 0)
    c = lax.broadcasted_iota(jnp.int32, (nrow, seq), 1) // MOBA_BLOCK
    avg = jnp.where(r == c, 1.0 / MOBA_BLOCK, 0.0).astype(BF16)
    kmean = jnp.dot(avg, k_ref[0], preferred_element_type=F32)
    khi = kmean.astype(BF16)
    klo = (kmean - khi.astype(F32)).astype(BF16)

    kk = lax.broadcasted_iota(jnp.int32, (tq, tq), 0)
    qq = lax.broadcasted_iota(jnp.int32, (tq, tq), 1)
    causal = kk <= qq
    blk = lax.broadcasted_iota(jnp.int32, (nrow, tq), 0)
    gout = gout_ref[...]
    pipe = _Pipe()

    for i in range(nb):
        qs = slice(i * tq, (i + 1) * tq)
        qms = _mask_heads(q_ref[0, qs, :])
        state = [None] * GROUP_HEADS

        selb = [None] * GROUP_HEADS
        if i > MOBA_TOPK:
            past = blk < i
            for h in range(GROUP_HEADS):
                g = jnp.where(past, _dot_nt(khi, qms[h]) + _dot_nt(klo, qms[h]), -jnp.inf)
                rank = jnp.zeros((nrow, tq), F32)
                for mth in range(i):
                    row = g[mth:mth + 1, :]
                    rank = rank + jnp.where(row == g, jnp.where(blk > mth, 1.0, 0.0),
                                            jnp.where(row > g, 1.0, 0.0))
                selb[h] = jnp.where(rank < MOBA_TOPK, 0.0, NEG)

        def finish_own(s, h, state=state, i=i):
            state[h] = _softmax_pv(jnp.where(causal, s, NEG), vt_ref[0, i, _head_slice(h), :])

        def finish_past(s, h, n, state=state, selb=selb):
            acc, m_old, l_old = state[h]
            smax = jnp.max(s, axis=0, keepdims=True)
            if selb[h] is None:
                m_new = jnp.maximum(m_old, smax)
                p = jnp.exp2(s - m_new)
            else:
                bias = selb[h][n:n + 1, :]
                m_new = jnp.maximum(m_old, smax + bias)
                p = jnp.exp2(s - (m_new - bias))
            alpha = jnp.exp2(m_old - m_new)
            pv = jnp.dot(vt_ref[0, n, _head_slice(h), :], p.astype(BF16), preferred_element_type=F32)
            state[h] = (alpha * acc + pv, m_new, alpha * l_old + jnp.sum(p, axis=0, keepdims=True))

        def finalize(state=state, qs=qs):
            ot = jnp.concatenate([acc / l for acc, _, l in state], axis=0)
            o_ref[0, qs, :] = _row_rms(ot.T, gout).astype(BF16)

        order = [i] + list(range(i))
        for pos, n in enumerate(order):
            ks = slice(n * tq, (n + 1) * tq)
            for h in range(GROUP_HEADS):
                score = lambda ks=ks, h=h, qms=qms: _dot_nt(k_ref[0, ks, :], qms[h])
                if n == i:
                    fin = lambda s, h=h, f=finish_own: f(s, h)
                else:
                    fin = lambda s, h=h, n=n, f=finish_past: f(s, h, n)
                if pos == len(order) - 1 and h == GROUP_HEADS - 1:
                    fin = lambda s, fin=fin, fz=finalize: (fin(s), fz())
                pipe.push(score, fin)
    pipe.flush()


def _moba(q, k, vt, gout):
    b, seq, gw = q.shape
    nb = seq // MOBA_BLOCK
    return pl.pallas_call(
        _moba_kernel,
        grid=(b,),
        in_specs=[pl.BlockSpec((1, seq, gw), lambda bi: (bi, 0, 0)),
                  pl.BlockSpec((1, seq, gw), lambda bi: (bi, 0, 0)),
                  pl.BlockSpec((1, nb, gw, MOBA_BLOCK), lambda bi: (bi, 0, 0, 0)),
                  pl.BlockSpec((1, gw), lambda bi: (0, 0))],
        out_specs=pl.BlockSpec((1, seq, gw), lambda bi: (bi, 0, 0)),
        out_shape=jax.ShapeDtypeStruct((b, seq, gw), BF16),
        compiler_params=_params(("parallel",)),
        name="moba",
    )(q, k, vt, gout)


def _dilated_kernel(qc_ref, k1_ref, kc_ref, v1_ref, vc_ref, b1_ref, b4_ref, b16_ref, gout_ref, o_ref,
                    acc_ref, m_ref, l_ref, v1t_ref, fin_ref):
    seq = k1_ref.shape[1]
    per = DIL_MAX // DIL_MID
    gout = gout_ref[...]
    pipe = _Pipe()

    def tile(q, k, vt, bias, done):
        qms = _mask_heads(q)
        parts = []

        def finish(s, h):
            parts.append(_softmax_pv(s, vt[_head_slice(h), :]))
            if h == GROUP_HEADS - 1:
                done(*_head_rows(parts))

        for h in range(GROUP_HEADS):
            pipe.push(lambda h=h: _dot_nt(k, qms[h]) + bias, lambda s, h=h: finish(s, h))

    for c0 in range(0, seq, 2 * BAND):
        v1t_ref[:, c0:c0 + 2 * BAND] = _transpose_bf16(v1_ref[0, c0:c0 + 2 * BAND, :])

    for r in range(DIL_MAX):
        def done16(acc, m_all, l_all, r=r):
            acc_ref[r] = acc
            m_ref[r] = m_all
            l_ref[r] = l_all

        tile(qc_ref[0, r], kc_ref[0, r], _transpose_bf16(vc_ref[0, r]), b16_ref[...], done16)

    jq4 = BAND // per
    for a in range(DIL_MID):
        classes = [u * DIL_MID + a for u in range(per)]
        for jb in range(seq // DIL_MID // BAND):
            rs = slice(jb * jq4, (jb + 1) * jq4)
            if jb == 0:
                ks = rs
                bias = jnp.concatenate([b4_ref[2 * jq4 * u + jq4:2 * jq4 * (u + 1), :] for u in range(per)], axis=0)
            else:
                ks = slice((jb - 1) * jq4, (jb + 1) * jq4)
                bias = b4_ref[...]

            def done4(a1, m1, l1, classes=classes, rs=rs):
                for u, c in enumerate(classes):
                    ts = slice(u * jq4, (u + 1) * jq4)
                    acc, m_all, l_all = _merge(acc_ref[c, rs, :], m_ref[c, rs, :], l_ref[c, rs, :],
                                               a1[ts], m1[ts], l1[ts])
                    acc_ref[c, rs, :] = acc
                    m_ref[c, rs, :] = m_all
                    l_ref[c, rs, :] = l_all

            v = jnp.concatenate([vc_ref[0, c, ks, :] for c in classes], axis=0)
            tile(jnp.concatenate([qc_ref[0, c, rs, :] for c in classes], axis=0),
                 jnp.concatenate([kc_ref[0, c, ks, :] for c in classes], axis=0),
                 _transpose_bf16(v), bias, done4)

    tq = 2 * BAND
    jq1 = tq // DIL_MAX
    for jb in range(seq // tq):
        rs = slice(jb * jq1, (jb + 1) * jq1)
        if jb == 0:
            ks = slice(0, tq)
            bias = b1_ref[BAND:, :]
        else:
            ks = slice(jb * tq - BAND, (jb + 1) * tq)
            bias = b1_ref[...]

        def done1(a1, m1, l1, rs=rs, jb=jb):
            for r in range(DIL_MAX):
                ts = slice(r * jq1, (r + 1) * jq1)
                acc, _, l_all = _merge(acc_ref[r, rs, :], m_ref[r, rs, :], l_ref[r, rs, :],
                                       a1[ts], m1[ts], l1[ts])
                res = _row_rms(acc / l_all, gout)
                for half in range(GROUP_WIDTH // BAND):
                    fin_ref[half, pl.ds(r, jq1, stride=DIL_MAX), :] = res[:, half * BAND:(half + 1) * BAND]
            for half in range(GROUP_WIDTH // BAND):
                o_ref[0, jb * tq:(jb + 1) * tq, half * BAND:(half + 1) * BAND] = fin_ref[half].astype(BF16)

        tile(jnp.concatenate([qc_ref[0, r, rs, :] for r in range(DIL_MAX)], axis=0),
             k1_ref[0, ks, :], v1t_ref[:, ks], bias, done1)
    pipe.flush()


def _dilated_biases():
    def band(krel, qrel):
        dist = qrel[None, :] - krel[:, None]
        return jnp.where((dist >= 0) & (dist <= BAND), 0.0, NEG).astype(F32)

    tq = 2 * BAND
    jq = tq // DIL_MAX
    col = jnp.arange(tq)
    b1 = band(jnp.arange(BAND + tq) - BAND, DIL_MAX * (col % jq) + col // jq)
    per = DIL_MAX // DIL_MID
    jq = BAND // per
    col = jnp.arange(BAND)
    row = jnp.arange(2 * BAND)
    b4 = band(per * (row % (2 * jq) - jq) + row // (2 * jq), per * (col % jq) + col // jq)
    b16 = band(jnp.arange(BAND), jnp.arange(BAND))
    return b1, b4, b16


def _dilated(qc, k1, kc, v1, vc, gout):
    b, seq, gw = k1.shape
    b1, b4, b16 = _dilated_biases()
    nat = pl.BlockSpec((1, seq, gw), lambda bi: (bi, 0, 0))
    cm_shape = (DIL_MAX, seq // DIL_MAX, gw)
    cm = pl.BlockSpec((1,) + cm_shape, lambda bi: (bi, 0, 0, 0))
    const = lambda shape: pl.BlockSpec(shape, lambda bi: (0, 0))
    return pl.pallas_call(
        _dilated_kernel,
        grid=(b,),
        in_specs=[cm, nat, cm, nat, cm, const(b1.shape), const(b4.shape), const(b16.shape), const((1, gw))],
        out_specs=nat,
        out_shape=jax.ShapeDtypeStruct((b, seq, gw), BF16),
        scratch_shapes=[pltpu.VMEM(cm_shape, F32)] * 3 + [pltpu.VMEM((gw, seq), BF16),
                                                          pltpu.VMEM((gw // BAND, 2 * BAND, BAND), F32)],
        compiler_params=_params(("parallel",)),
        name="dilated",
    )(qc, k1, kc, v1, vc, b1, b4, b16, gout)


CONV_PAD = 32
CONV_ROWS = 256


def _conv_kernel(u_ref, w_ref, b_ref, lng_ref, lnb_ref, wout_ref, gout_ref, o_ref, pad_ref):
    seq = u_ref.shape[1]
    pad_ref[0:CONV_PAD, :] = jnp.zeros((CONV_PAD, GROUP_WIDTH), F32)
    pad_ref[CONV_PAD:CONV_PAD + seq, :] = u_ref[0].astype(F32)
    pad_ref[CONV_PAD + seq:, :] = jnp.zeros((16, GROUP_WIDTH), F32)
    lead = CONV_PAD - (CONV_WIDTH - 1)
    span = CONV_ROWS + 16
    for c0 in range(0, seq, CONV_ROWS):
        y = jnp.broadcast_to(b_ref[...], (CONV_ROWS, GROUP_WIDTH))
        for sub in range(8):
            z = None
            for a in range((CONV_WIDTH - sub + 7) // 8):
                j = 8 * a + sub
                t = w_ref[j:j + 1, :] * pad_ref[c0 + 8 * a:c0 + 8 * a + span, :]
                z = t if z is None else z + t
            y = y + z[lead + sub:lead + sub + CONV_ROWS]
        mu = jnp.mean(y, axis=-1, keepdims=True)
        d = y - mu
        var = jnp.mean(d * d, axis=-1, keepdims=True)
        yn = d * lax.rsqrt(var + EPS) * lng_ref[...] + lnb_ref[...]
        act = yn * (1.0 / (1.0 + jnp.exp(-yn)))
        oc = jnp.dot(act.astype(BF16), wout_ref[...], preferred_element_type=F32)
        o_ref[0, c0:c0 + CONV_ROWS, :] = _row_rms(oc, gout_ref[...]).astype(BF16)


def _conv_module(u, w, bias, lng, lnb, wout, gout):
    b, seq, gw = u.shape
    const = lambda shape: pl.BlockSpec(shape, lambda bi: (0, 0))
    return pl.pallas_call(
        _conv_kernel,
        grid=(b,),
        in_specs=[pl.BlockSpec((1, seq, gw), lambda bi: (bi, 0, 0)),
                  const((CONV_PAD, gw)), const((1, gw)), const((1, gw)), const((1, gw)),
                  const((gw, gw)), const((1, gw))],
        out_specs=pl.BlockSpec((1, seq, gw), lambda bi: (bi, 0, 0)),
        out_shape=jax.ShapeDtypeStruct((b, seq, gw), BF16),
        scratch_shapes=[pltpu.VMEM((CONV_PAD + seq + 16, gw), F32)],
        compiler_params=_params(("parallel",)),
        name="conv_module",
    )(u, w, bias, lng, lnb, wout, gout)


MEM_QROWS = 256


def _mem_kernel(q_ref, mem_ref, gmem_ref, wkv_ref, gk_ref, gout_ref, o_ref):
    seq = q_ref.shape[1]
    mh = _row_rms(mem_ref[0], gmem_ref[...]).astype(BF16)
    kv = jnp.dot(mh, wkv_ref[...], preferred_element_type=F32)
    km = _head_rms(kv[:, :GROUP_WIDTH], gk_ref[...], _head_ones()).astype(BF16)
    vmt = kv[:, GROUP_WIDTH:].T.astype(BF16)
    gout = gout_ref[...]
    pipe = _Pipe()
    for c0 in range(0, seq, MEM_QROWS):
        qms = _mask_heads(q_ref[0, c0:c0 + MEM_QROWS, :])
        parts = []

        def finish(s, h, parts=parts, c0=c0):
            parts.append(_softmax_pv(s, vmt[_head_slice(h), :]))
            if h == GROUP_HEADS - 1:
                ot = jnp.concatenate([o / l for o, _, l in parts], axis=0)
                o_ref[0, c0:c0 + MEM_QROWS, :] = _row_rms(ot.T, gout).astype(BF16)

        for h in range(GROUP_HEADS):
            pipe.push(lambda h=h, qms=qms: _dot_nt(km, qms[h]), lambda s, h=h, f=finish: f(s, h))
    pipe.flush()


def _mem_attn(q, mem, gmem, wkv, gk, gout):
    b, seq, gw = q.shape
    n_mem = mem.shape[1]
    const = lambda shape: pl.BlockSpec(shape, lambda bi: (0, 0))
    return pl.pallas_call(
        _mem_kernel,
        grid=(b,),
        in_specs=[pl.BlockSpec((1, seq, gw), lambda bi: (bi, 0, 0)),
                  pl.BlockSpec((1, n_mem, D_MODEL), lambda bi: (bi, 0, 0)),
                  const((1, D_MODEL)), const((D_MODEL, 2 * gw)), const((1, gw)), const((1, gw))],
        out_specs=pl.BlockSpec((1, seq, gw), lambda bi: (bi, 0, 0)),
        out_shape=jax.ShapeDtypeStruct((b, seq, gw), BF16),
        compiler_params=_params(("parallel",)),
        name="mem_attn",
    )(q, mem, gmem, wkv, gk, gout)


def _out_proj_kernel(x_ref, oa_ref, ob_ref, oc_ref, om_ref, w_ref, o_ref):
    acc = x_ref[...]
    for g, ref in enumerate((oa_ref, ob_ref, oc_ref, om_ref)):
        acc = acc + jnp.dot(ref[...], w_ref[g * GROUP_WIDTH:(g + 1) * GROUP_WIDTH, :],
                            preferred_element_type=F32)
    o_ref[...] = acc


def _out_proj(x2, oa, ob, oc, om, w):
    rows = x2.shape[0]
    tm = ROW_TILE
    gspec = pl.BlockSpec((tm, GROUP_WIDTH), lambda i: (i, 0))
    xspec = pl.BlockSpec((tm, D_MODEL), lambda i: (i, 0))
    return pl.pallas_call(
        _out_proj_kernel,
        grid=(rows // tm,),
        in_specs=[xspec, gspec, gspec, gspec, gspec, pl.BlockSpec((D_MODEL, D_MODEL), lambda i: (0, 0))],
        out_specs=xspec,
        out_shape=jax.ShapeDtypeStruct((rows, D_MODEL), F32),
        compiler_params=_params(("parallel",)),
        name="out_proj",
    )(x2, oa, ob, oc, om, w)


def _ffn_kernel(pos_blocks, x_ref, xh_ref, g_ref, wup_ref, cw_ref, cb_ref, wdown_ref, o_ref, h_ref, act_ref):
    tm = x_ref.shape[0]
    x = x_ref[...]
    keep = jnp.where(pl.program_id(0) % pos_blocks == 0, 0.0, 1.0)
    h_ref[0:HALO, :] = (_row_rms(xh_ref[...], g_ref[...]) * keep).astype(BF16)
    h_ref[HALO:, :] = _row_rms(x, g_ref[...]).astype(BF16)
    h = h_ref[...]

    def conv(col0):
        cs = slice(col0, col0 + FFN_CHUNK)
        u = jnp.dot(h, wup_ref[:, cs], preferred_element_type=F32)
        y = cb_ref[:, cs] + cw_ref[2:3, cs] * u[HALO:HALO + tm]
        y = y + cw_ref[1:2, cs] * u[HALO - 1:HALO - 1 + tm]
        return y + cw_ref[0:1, cs] * u[HALO - 2:HALO - 2 + tm]

    for c in range(D_FF // FFN_CHUNK):
        gate = conv(c * FFN_CHUNK)
        val = conv(D_FF + c * FFN_CHUNK)
        act = gate * (1.0 / (1.0 + jnp.exp(-gate))) * val
        act_ref[:, c * FFN_CHUNK:(c + 1) * FFN_CHUNK] = act.astype(BF16)
    o_ref[...] = x + jnp.dot(act_ref[...], wdown_ref[...], preferred_element_type=F32)


def _ffn(x2, g, wup, cw, cb, wdown, seq):
    rows = x2.shape[0]
    tm = ROW_TILE
    pos_blocks = seq // tm
    hb = tm // HALO
    const = lambda shape: pl.BlockSpec(shape, lambda i: (0, 0))
    xspec = pl.BlockSpec((tm, D_MODEL), lambda i: (i, 0))
    return pl.pallas_call(
        functools.partial(_ffn_kernel, pos_blocks),
        grid=(rows // tm,),
        in_specs=[xspec,
                  pl.BlockSpec((HALO, D_MODEL), lambda i: (jnp.maximum(i * hb - 1, 0), 0)),
                  const((1, D_MODEL)), const((D_MODEL, 2 * D_FF)), const((8, 2 * D_FF)),
                  const((1, 2 * D_FF)), const((D_FF, D_MODEL))],
        out_specs=xspec,
        out_shape=jax.ShapeDtypeStruct((rows, D_MODEL), F32),
        scratch_shapes=[pltpu.VMEM((HALO + tm, D_MODEL), BF16), pltpu.VMEM((tm, D_FF), BF16)],
        compiler_params=_params(("parallel",)),
        name="ffn",
    )(x2, x2, g, wup, cw, cb, wdown)


def _rope_tables(seq):
    inv = ROPE_THETA ** (-jnp.arange(0, HEAD_DIM, 2, dtype=F32) / HEAD_DIM)
    ang = jnp.arange(seq, dtype=F32)[:, None] * inv[None, :]
    ang = jnp.concatenate([ang, ang], axis=-1)
    sign = jnp.where(jnp.arange(HEAD_DIM) < HEAD_DIM // 2, -1.0, 1.0).astype(F32)
    cos = jnp.tile(jnp.cos(ang), (1, GROUP_HEADS))
    sin = jnp.tile(jnp.sin(ang) * sign[None, :], (1, GROUP_HEADS))
    return cos, sin


def _pad_rows(a, rows):
    return jnp.concatenate([a, jnp.zeros((rows - a.shape[0],) + a.shape[1:], a.dtype)], axis=0)


def kernel(x, mem, norm_mix, w_in, q_norm_a, k_norm_a, q_norm_b, k_norm_b, q_norm_m, k_norm_m,
           mem_norm, w_mem_kv, conv_w, conv_b, conv_ln_g, conv_ln_b, w_conv_out, out_norm, w_out,
           norm_ffn, w_up, ffn_conv_w, ffn_conv_b, w_down):
    b, seq, d = x.shape
    depth = w_in.shape[0]
    gw = GROUP_WIDTH
    cos, sin = _rope_tables(seq)
    x2 = x.reshape(b * seq, d)
    tile_h = lambda g: jnp.tile(g, GROUP_HEADS)
    r3 = lambda t: t.reshape(b, seq, gw)
    f2 = lambda t: t.reshape(b * seq, gw)
    for i in range(depth):
        gains = _pad_rows(jnp.stack([tile_h(q_norm_a[i]), tile_h(k_norm_a[i]), tile_h(q_norm_b[i]),
                                     tile_h(k_norm_b[i]), tile_h(q_norm_m[i])]), 8)
        w = w_in[i].astype(BF16)
        wvat = w[:, 2 * gw:3 * gw].T
        qa, ka, vat, qbc, kb, kbc, vb, vbc, u, qm = _in_proj(x2, norm_mix[i][None], w, wvat, cos, sin, gains, seq)
        gout = out_norm[i].reshape(4, 1, gw)
        oa = _moba(r3(qa), r3(ka), vat, gout[0])
        ob = _dilated(qbc, r3(kb), kbc, r3(vb), vbc, gout[1])
        oc = _conv_module(r3(u), _pad_rows(conv_w[i], CONV_PAD), conv_b[i][None], conv_ln_g[i][None],
                          conv_ln_b[i][None], w_conv_out[i].astype(BF16), gout[2])
        om = _mem_attn(r3(qm), mem, mem_norm[i][None], w_mem_kv[i].astype(BF16),
                       tile_h(k_norm_m[i])[None], gout[3])
        x2 = _out_proj(x2, f2(oa), f2(ob), f2(oc), f2(om), w_out[i].astype(BF16))
        x2 = _ffn(x2, norm_ffn[i][None], w_up[i].astype(BF16), _pad_rows(ffn_conv_w[i], 8),
                  ffn_conv_b[i][None], w_down[i].astype(BF16), seq)
    return x2.reshape(b, seq, d)
```

```python
import functools

import jax
import jax.numpy as jnp
from jax import lax
from jax.experimental import pallas as pl
from jax.experimental.pallas import tpu as pltpu

F32 = jnp.float32
BF16 = jnp.bfloat16

D_MODEL = 1024
HEAD_DIM = 64
GROUP_WIDTH = 256
GROUP_HEADS = 4
N_PROJ_GROUPS = 9
MOBA_BLOCK = 256
MOBA_TOPK = 3
BAND = 128
DIL_MID = 4
DIL_MAX = 16
CONV_WIDTH = 31
FFN_CONV_WIDTH = 3
D_FF = 2816
ROPE_THETA = 10000.0
EPS = 1e-6
ATTN_SCALE = HEAD_DIM ** -0.5
Q_SCALE = ATTN_SCALE * 1.4426950408889634
NEG = -1e30

ROW_TILE = 512
FFN_CHUNK = 256
HALO = 16
PIPE_DEPTH = 3
PROJ_DEPTH = 2
STAT_ROWS = 8
VMEM_LIMIT = 56 * 1024 * 1024


def _params(sem):
    return pltpu.CompilerParams(dimension_semantics=sem, vmem_limit_bytes=VMEM_LIMIT)


def _lane_head(shape):
    return lax.broadcasted_iota(jnp.int32, shape, 1) // HEAD_DIM


def _head_ones():
    r = lax.broadcasted_iota(jnp.int32, (GROUP_WIDTH, GROUP_WIDTH), 0) // HEAD_DIM
    c = lax.broadcasted_iota(jnp.int32, (GROUP_WIDTH, GROUP_WIDTH), 1) // HEAD_DIM
    return jnp.where(r == c, 1.0, 0.0).astype(BF16)


def _head_rms(p, gain, ones_bd):
    ss = jnp.dot((p * p).astype(BF16), ones_bd, preferred_element_type=F32)
    return p * lax.rsqrt(ss * (1.0 / HEAD_DIM) + EPS) * gain


def _row_rms(x, gain):
    return x * lax.rsqrt(jnp.mean(x * x, axis=-1, keepdims=True) + EPS) * gain


def _dot_nt(a, b):
    return lax.dot_general(a, b, (((1,), (1,)), ((), ())), preferred_element_type=F32)


def _stack_heads(k):
    lh = _lane_head((1, GROUP_WIDTH))
    return jnp.concatenate([k * jnp.where(lh == h, 1.0, 0.0).astype(BF16) for h in range(GROUP_HEADS)], axis=0)


def _head_slice(h):
    return slice(h * HEAD_DIM, (h + 1) * HEAD_DIM)


def _transpose_bf16(v):
    return v.astype(F32).T.astype(BF16)


class _Pipe:
    def __init__(self, depth=3):
        self.depth = depth
        self.items = []

    def push(self, score, finish):
        self.items.append((score(), finish))
        if len(self.items) > self.depth:
            s, f = self.items.pop(0)
            f(s)

    def flush(self):
        for s, f in self.items:
            f(s)
        self.items = []


def _exp_pv(s, shift, vt_h):
    p = jnp.exp2(s - shift).astype(BF16)
    ones = jnp.ones((2 * STAT_ROWS, vt_h.shape[1]), BF16)
    pv = jnp.dot(jnp.concatenate([vt_h, ones], axis=0), p, preferred_element_type=F32)
    return pv[:HEAD_DIM], pv[HEAD_DIM:HEAD_DIM + 1]


def _softmax_pv(s, vt_h):
    m = jnp.max(s, axis=0, keepdims=True)
    o, l = _exp_pv(s, m, vt_h)
    return o, m, l


def _head_rows(parts):
    tq = parts[0][0].shape[1]
    rows = lambda xs: jnp.concatenate(xs, axis=0).T
    return (rows([o for o, _, _ in parts]),
            rows([jnp.broadcast_to(m, (HEAD_DIM, tq)) for _, m, _ in parts]),
            rows([jnp.broadcast_to(l, (HEAD_DIM, tq)) for _, _, l in parts]))


def _merge(acc0, m0, l0, acc1, m1, l1):
    mn = jnp.maximum(m0, m1)
    a0 = jnp.exp2(m0 - mn)
    a1 = jnp.exp2(m1 - mn)
    return acc0 * a0 + acc1 * a1, mn, l0 * a0 + l1 * a1


def _in_proj_kernel(x_ref, g_ref, w_ref, wvat_ref, cos_ref, sin_ref, gains_ref,
                    qa_ref, ka_ref, vat_ref, qbc_ref, kb_ref, kbc_ref, vb_ref, vbc_ref, u_ref, qm_ref,
                    perm_ref):
    h = _row_rms(x_ref[...], g_ref[...]).astype(BF16)
    ones_bd = _head_ones()
    cos = cos_ref[...]
    sin = sin_ref[...]
    tm = h.shape[0]
    lower = (lax.broadcasted_iota(jnp.int32, (tm, GROUP_WIDTH), 1) % HEAD_DIM) < (HEAD_DIM // 2)

    def proj(j):
        return jnp.dot(h, w_ref[:, j * GROUP_WIDTH:(j + 1) * GROUP_WIDTH], preferred_element_type=F32)

    def rope(y):
        rot = jnp.where(lower, pltpu.roll(y, GROUP_WIDTH - HEAD_DIM // 2, 1), pltpu.roll(y, HEAD_DIM // 2, 1))
        return y * cos + rot * sin

    def class_major(y, out_ref):
        for half in range(GROUP_WIDTH // BAND):
            ls = slice(half * BAND, (half + 1) * BAND)
            perm_ref[half] = y[:, ls]
            for r in range(DIL_MAX):
                out_ref[0, r, :, ls] = perm_ref[half, pl.ds(r, tm // DIL_MAX, stride=DIL_MAX), :].astype(BF16)

    def head_norm(y, gain_row):
        return _head_rms(y, gains_ref[gain_row:gain_row + 1, :], ones_bd)

    def fin_qa(y):
        qa_ref[...] = (rope(head_norm(y, 0)) * Q_SCALE).astype(BF16)

    def fin_ka(y):
        ka_ref[...] = rope(head_norm(y, 1)).astype(BF16)

    def fin_vat(y):
        vat = y.astype(BF16)
        for j in range(tm // MOBA_BLOCK):
            vat_ref[0, j] = vat[:, j * MOBA_BLOCK:(j + 1) * MOBA_BLOCK]

    def fin_qb(y):
        class_major(rope(head_norm(y, 2)) * Q_SCALE, qbc_ref)

    def fin_kb(y):
        kb = rope(head_norm(y, 3))
        kb_ref[...] = kb.astype(BF16)
        class_major(kb, kbc_ref)

    def fin_vb(y):
        vb_ref[...] = y.astype(BF16)
        class_major(y, vbc_ref)

    def fin_glu(ys):
        c_val, c_gate = ys
        u_ref[...] = (c_val * (1.0 / (1.0 + jnp.exp(-c_gate)))).astype(BF16)

    def fin_qm(y):
        qm_ref[...] = (head_norm(y, 4) * Q_SCALE).astype(BF16)

    pipe = _Pipe(PROJ_DEPTH)
    pipe.push(lambda: proj(0), fin_qa)
    pipe.push(lambda: proj(1), fin_ka)
    pipe.push(lambda: _dot_nt(wvat_ref[...], h), fin_vat)
    pipe.push(lambda: proj(3), fin_qb)
    pipe.push(lambda: proj(4), fin_kb)
    pipe.push(lambda: proj(5), fin_vb)
    pipe.push(lambda: (proj(6), proj(7)), fin_glu)
    pipe.push(lambda: proj(8), fin_qm)
    pipe.flush()


def _in_proj(x2, g, w, wvat, cos, sin, gains, seq):
    rows = x2.shape[0]
    b = rows // seq
    tm = ROW_TILE
    pos_blocks = seq // tm
    nat = jax.ShapeDtypeStruct((rows, GROUP_WIDTH), BF16)
    nat_spec = pl.BlockSpec((tm, GROUP_WIDTH), lambda i: (i, 0))
    cm = jax.ShapeDtypeStruct((b, DIL_MAX, seq // DIL_MAX, GROUP_WIDTH), BF16)
    cm_spec = pl.BlockSpec((1, DIL_MAX, tm // DIL_MAX, GROUP_WIDTH),
                           lambda i: (i // pos_blocks, 0, i % pos_blocks, 0))
    vat = jax.ShapeDtypeStruct((b, seq // MOBA_BLOCK, GROUP_WIDTH, MOBA_BLOCK), BF16)
    vat_spec = pl.BlockSpec((1, tm // MOBA_BLOCK, GROUP_WIDTH, MOBA_BLOCK),
                            lambda i: (i // pos_blocks, i % pos_blocks, 0, 0))
    return pl.pallas_call(
        _in_proj_kernel,
        grid=(rows // tm,),
        in_specs=[
            pl.BlockSpec((tm, D_MODEL), lambda i: (i, 0)),
            pl.BlockSpec((1, D_MODEL), lambda i: (0, 0)),
            pl.BlockSpec((D_MODEL, N_PROJ_GROUPS * GROUP_WIDTH), lambda i: (0, 0)),
            pl.BlockSpec((GROUP_WIDTH, D_MODEL), lambda i: (0, 0)),
            pl.BlockSpec((tm, GROUP_WIDTH), lambda i: (i % pos_blocks, 0)),
            pl.BlockSpec((tm, GROUP_WIDTH), lambda i: (i % pos_blocks, 0)),
            pl.BlockSpec((8, GROUP_WIDTH), lambda i: (0, 0)),
        ],
        out_specs=[nat_spec, nat_spec, vat_spec, cm_spec, nat_spec, cm_spec, nat_spec, cm_spec, nat_spec, nat_spec],
        out_shape=[nat, nat, vat, cm, nat, cm, nat, cm, nat, nat],
        scratch_shapes=[pltpu.VMEM((GROUP_WIDTH // BAND, tm, BAND), F32)],
        compiler_params=_params(("parallel",)),
        name="in_proj",
    )(x2, g, w, wvat, cos, sin, gains)


def _moba_kernel(q_ref, k_ref, vt_ref, gout_ref, o_ref):
    seq = k_ref.shape[1]
    nb = seq // MOBA_BLOCK
    tq = MOBA_BLOCK
    nrow = 2 * STAT_ROWS

    r = lax.broadcasted_iota(jnp.int32, (nrow, seq), 0)
    c = lax.broadcasted_iota(jnp.int32, (nrow, seq), 1) // MOBA_BLOCK
    avg = jnp.where(r == c, 1.0 / MOBA_BLOCK, 0.0).astype(BF16)
    kmean = jnp.dot(avg, k_ref[0], preferred_element_type=F32)
    khi = kmean.astype(BF16)
    klo = (kmean - khi.astype(F32)).astype(BF16)
    khi_st = _stack_heads(khi)
    klo_st = _stack_heads(klo)

    kk = lax.broadcasted_iota(jnp.int32, (tq, tq), 0)
    qq = lax.broadcasted_iota(jnp.int32, (tq, tq), 1)
    causal = kk <= qq
    blk = lax.broadcasted_iota(jnp.int32, (nrow, tq), 0)
    gout = gout_ref[...]
    pipe = _Pipe(PIPE_DEPTH)
    kst = {}

    def stacked(n):
        if n not in kst:
            kst[n] = _stack_heads(k_ref[0, n * tq:(n + 1) * tq, :])
        return kst[n]

    for i in range(nb):
        qs = slice(i * tq, (i + 1) * tq)
        q = q_ref[0, qs, :]
        state = [None] * GROUP_HEADS

        selb = [None] * GROUP_HEADS
        if i > MOBA_TOPK:
            past = blk < i
            g_all = _dot_nt(khi_st, q) + _dot_nt(klo_st, q)
            for h in range(GROUP_HEADS):
                g = jnp.where(past, g_all[h * nrow:(h + 1) * nrow], -jnp.inf)
                rank = jnp.zeros((nrow, tq), F32)
                for mth in range(i):
                    row = g[mth:mth + 1, :]
                    rank = rank + jnp.where(row == g, jnp.where(blk > mth, 1.0, 0.0),
                                            jnp.where(row > g, 1.0, 0.0))
                selb[h] = jnp.where(rank < MOBA_TOPK, 0.0, NEG)

        def finish_own(s_all, state=state, i=i):
            for h in range(GROUP_HEADS):
                s = jnp.where(causal, s_all[h * tq:(h + 1) * tq], NEG)
                state[h] = _softmax_pv(s, vt_ref[0, i, _head_slice(h), :])

        def finish_past(s_all, n, state=state, selb=selb):
            for h in range(GROUP_HEADS):
                s = s_all[h * tq:(h + 1) * tq]
                acc, m_old, l_old = state[h]
                smax = jnp.max(s, axis=0, keepdims=True)
                if selb[h] is None:
                    m_new = jnp.maximum(m_old, smax)
                    shift = m_new
                else:
                    bias = selb[h][n:n + 1, :]
                    m_new = jnp.maximum(m_old, smax + bias)
                    shift = m_new - bias
                alpha = jnp.exp2(m_old - m_new)
                pv, l_new = _exp_pv(s, shift, vt_ref[0, n, _head_slice(h), :])
                state[h] = (alpha * acc + pv, m_new, alpha * l_old + l_new)

        def finalize(state=state, qs=qs):
            ot = jnp.concatenate([acc / l for acc, _, l in state], axis=0)
            o_ref[0, qs, :] = _row_rms(ot.T, gout).astype(BF16)

        order = [i] + list(range(i))
        for pos, n in enumerate(order):
            score = lambda n=n, q=q: _dot_nt(stacked(n), q)
            if n == i:
                fin = finish_own
            else:
                fin = lambda s, n=n, f=finish_past: f(s, n)
            if pos == len(order) - 1:
                fin = lambda s, fin=fin, fz=finalize: (fin(s), fz())
            pipe.push(score, fin)
    pipe.flush()


def _moba(q, k, vt, gout):
    b, seq, gw = q.shape
    nb = seq // MOBA_BLOCK
    return pl.pallas_call(
        _moba_kernel,
        grid=(b,),
        in_specs=[pl.BlockSpec((1, seq, gw), lambda bi: (bi, 0, 0)),
                  pl.BlockSpec((1, seq, gw), lambda bi: (bi, 0, 0)),
                  pl.BlockSpec((1, nb, gw, MOBA_BLOCK), lambda bi: (bi, 0, 0, 0)),
                  pl.BlockSpec((1, gw), lambda bi: (0, 0))],
        out_specs=pl.BlockSpec((1, seq, gw), lambda bi: (bi, 0, 0)),
        out_shape=jax.ShapeDtypeStruct((b, seq, gw), BF16),
        compiler_params=_params(("parallel",)),
        name="moba",
    )(q, k, vt, gout)


def _dilated_kernel(qc_ref, k1_ref, kc_ref, v1_ref, vc_ref, b1_ref, b4_ref, b16_ref, gout_ref, o_ref,
                    acc_ref, m_ref, l_ref, v1t_ref, fin_ref):
    seq = k1_ref.shape[1]
    per = DIL_MAX // DIL_MID
    gout = gout_ref[...]
    pipe = _Pipe(PIPE_DEPTH)

    def tile(q, k, vt, bias, done):
        tk = k.shape[0]

        def finish(s_all):
            parts = [_softmax_pv(s_all[h * tk:(h + 1) * tk] + bias, vt[_head_slice(h), :])
                     for h in range(GROUP_HEADS)]
            done(*_head_rows(parts))

        pipe.push(lambda: _dot_nt(_stack_heads(k), q), finish)

    for c0 in range(0, seq, 2 * BAND):
        v1t_ref[:, c0:c0 + 2 * BAND] = _transpose_bf16(v1_ref[0, c0:c0 + 2 * BAND, :])

    for r in range(DIL_MAX):
        def done16(acc, m_all, l_all, r=r):
            acc_ref[r] = acc
            m_ref[r] = m_all
            l_ref[r] = l_all

        tile(qc_ref[0, r], kc_ref[0, r], _transpose_bf16(vc_ref[0, r]), b16_ref[...], done16)

    jq4 = BAND // per
    for a in range(DIL_MID):
        classes = [u * DIL_MID + a for u in range(per)]
        for jb in range(seq // DIL_MID // BAND):
            rs = slice(jb * jq4, (jb + 1) * jq4)
            if jb == 0:
                ks = rs
                bias = jnp.concatenate([b4_ref[2 * jq4 * u + jq4:2 * jq4 * (u + 1), :] for u in range(per)], axis=0)
            else:
                ks = slice((jb - 1) * jq4, (jb + 1) * jq4)
                bias = b4_ref[...]

            def done4(a1, m1, l1, classes=classes, rs=rs):
                for u, c in enumerate(classes):
                    ts = slice(u * jq4, (u + 1) * jq4)
                    acc, m_all, l_all = _merge(acc_ref[c, rs, :], m_ref[c, rs, :], l_ref[c, rs, :],
                                               a1[ts], m1[ts], l1[ts])
                    acc_ref[c, rs, :] = acc
                    m_ref[c, rs, :] = m_all
                    l_ref[c, rs, :] = l_all

            v = jnp.concatenate([vc_ref[0, c, ks, :] for c in classes], axis=0)
            tile(jnp.concatenate([qc_ref[0, c, rs, :] for c in classes], axis=0),
                 jnp.concatenate([kc_ref[0, c, ks, :] for c in classes], axis=0),
                 _transpose_bf16(v), bias, done4)

    tq = 2 * BAND
    jq1 = tq // DIL_MAX
    for jb in range(seq // tq):
        rs = slice(jb * jq1, (jb + 1) * jq1)
        if jb == 0:
            ks = slice(0, tq)
            bias = b1_ref[BAND:, :]
        else:
            ks = slice(jb * tq - BAND, (jb + 1) * tq)
            bias = b1_ref[...]

        def done1(a1, m1, l1, rs=rs, jb=jb):
            for r in range(DIL_MAX):
                ts = slice(r * jq1, (r + 1) * jq1)
                acc, _, l_all = _merge(acc_ref[r, rs, :], m_ref[r, rs, :], l_ref[r, rs, :],
                                       a1[ts], m1[ts], l1[ts])
                res = _row_rms(acc / l_all, gout)
                for half in range(GROUP_WIDTH // BAND):
                    fin_ref[half, pl.ds(r, jq1, stride=DIL_MAX), :] = res[:, half * BAND:(half + 1) * BAND]
            for half in range(GROUP_WIDTH // BAND):
                o_ref[0, jb * tq:(jb + 1) * tq, half * BAND:(half + 1) * BAND] = fin_ref[half].astype(BF16)

        tile(jnp.concatenate([qc_ref[0, r, rs, :] for r in range(DIL_MAX)], axis=0),
             k1_ref[0, ks, :], v1t_ref[:, ks], bias, done1)
    pipe.flush()


def _dilated_biases():
    def band(krel, qrel):
        dist = qrel[None, :] - krel[:, None]
        return jnp.where((dist >= 0) & (dist <= BAND), 0.0, NEG).astype(F32)

    tq = 2 * BAND
    jq = tq // DIL_MAX
    col = jnp.arange(tq)
    b1 = band(jnp.arange(BAND + tq) - BAND, DIL_MAX * (col % jq) + col // jq)
    per = DIL_MAX // DIL_MID
    jq = BAND // per
    col = jnp.arange(BAND)
    row = jnp.arange(2 * BAND)
    b4 = band(per * (row % (2 * jq) - jq) + row // (2 * jq), per * (col % jq) + col // jq)
    b16 = band(jnp.arange(BAND), jnp.arange(BAND))
    return b1, b4, b16


def _dilated(qc, k1, kc, v1, vc, gout):
    b, seq, gw = k1.shape
    b1, b4, b16 = _dilated_biases()
    nat = pl.BlockSpec((1, seq, gw), lambda bi: (bi, 0, 0))
    cm_shape = (DIL_MAX, seq // DIL_MAX, gw)
    cm = pl.BlockSpec((1,) + cm_shape, lambda bi: (bi, 0, 0, 0))
    const = lambda shape: pl.BlockSpec(shape, lambda bi: (0, 0))
    return pl.pallas_call(
        _dilated_kernel,
        grid=(b,),
        in_specs=[cm, nat, cm, nat, cm, const(b1.shape), const(b4.shape), const(b16.shape), const((1, gw))],
        out_specs=nat,
        out_shape=jax.ShapeDtypeStruct((b, seq, gw), BF16),
        scratch_shapes=[pltpu.VMEM(cm_shape, F32)] * 3 + [pltpu.VMEM((gw, seq), BF16),
                                                          pltpu.VMEM((gw // BAND, 2 * BAND, BAND), F32)],
        compiler_params=_params(("parallel",)),
        name="dilated",
    )(qc, k1, kc, v1, vc, b1, b4, b16, gout)


CONV_PAD = 32
CONV_ROWS = 256


def _conv_kernel(u_ref, w_ref, b_ref, lng_ref, lnb_ref, wout_ref, gout_ref, o_ref, pad_ref):
    seq = u_ref.shape[1]
    pad_ref[0:CONV_PAD, :] = jnp.zeros((CONV_PAD, GROUP_WIDTH), F32)
    pad_ref[CONV_PAD:CONV_PAD + seq, :] = u_ref[0].astype(F32)
    pad_ref[CONV_PAD + seq:, :] = jnp.zeros((16, GROUP_WIDTH), F32)
    lead = CONV_PAD - (CONV_WIDTH - 1)
    span = CONV_ROWS + 16
    for c0 in range(0, seq, CONV_ROWS):
        y = jnp.broadcast_to(b_ref[...], (CONV_ROWS, GROUP_WIDTH))
        for sub in range(8):
            z = None
            for a in range((CONV_WIDTH - sub + 7) // 8):
                j = 8 * a + sub
                t = w_ref[j:j + 1, :] * pad_ref[c0 + 8 * a:c0 + 8 * a + span, :]
                z = t if z is None else z + t
            y = y + z[lead + sub:lead + sub + CONV_ROWS]
        mu = jnp.mean(y, axis=-1, keepdims=True)
        d = y - mu
        var = jnp.mean(d * d, axis=-1, keepdims=True)
        yn = d * lax.rsqrt(var + EPS) * lng_ref[...] + lnb_ref[...]
        act = yn * (1.0 / (1.0 + jnp.exp(-yn)))
        oc = jnp.dot(act.astype(BF16), wout_ref[...], preferred_element_type=F32)
        o_ref[0, c0:c0 + CONV_ROWS, :] = _row_rms(oc, gout_ref[...]).astype(BF16)


def _conv_module(u, w, bias, lng, lnb, wout, gout):
    b, seq, gw = u.shape
    const = lambda shape: pl.BlockSpec(shape, lambda bi: (0, 0))
    return pl.pallas_call(
        _conv_kernel,
        grid=(b,),
        in_specs=[pl.BlockSpec((1, seq, gw), lambda bi: (bi, 0, 0)),
                  const((CONV_PAD, gw)), const((1, gw)), const((1, gw)), const((1, gw)),
                  const((gw, gw)), const((1, gw))],
        out_specs=pl.BlockSpec((1, seq, gw), lambda bi: (bi, 0, 0)),
        out_shape=jax.ShapeDtypeStruct((b, seq, gw), BF16),
        scratch_shapes=[pltpu.VMEM((CONV_PAD + seq + 16, gw), F32)],
        compiler_params=_params(("parallel",)),
        name="conv_module",
    )(u, w, bias, lng, lnb, wout, gout)


MEM_QROWS = 256


def _mem_kernel(q_ref, mem_ref, gmem_ref, wkv_ref, gk_ref, gout_ref, o_ref):
    seq = q_ref.shape[1]
    mh = _row_rms(mem_ref[0], gmem_ref[...]).astype(BF16)
    kv = jnp.dot(mh, wkv_ref[...], preferred_element_type=F32)
    km = _head_rms(kv[:, :GROUP_WIDTH], gk_ref[...], _head_ones()).astype(BF16)
    vmt = kv[:, GROUP_WIDTH:].T.astype(BF16)
    gout = gout_ref[...]
    pipe = _Pipe(PIPE_DEPTH)
    km_st = _stack_heads(km)
    n_mem = km.shape[0]
    for c0 in range(0, seq, MEM_QROWS):
        def finish(s_all, c0=c0):
            parts = [_softmax_pv(s_all[h * n_mem:(h + 1) * n_mem], vmt[_head_slice(h), :])
                     for h in range(GROUP_HEADS)]
            ot = jnp.concatenate([o / l for o, _, l in parts], axis=0)
            o_ref[0, c0:c0 + MEM_QROWS, :] = _row_rms(ot.T, gout).astype(BF16)

        pipe.push(lambda c0=c0: _dot_nt(km_st, q_ref[0, c0:c0 + MEM_QROWS, :]), finish)
    pipe.flush()


def _mem_attn(q, mem, gmem, wkv, gk, gout):
    b, seq, gw = q.shape
    n_mem = mem.shape[1]
    const = lambda shape: pl.BlockSpec(shape, lambda bi: (0, 0))
    return pl.pallas_call(
        _mem_kernel,
        grid=(b,),
        in_specs=[pl.BlockSpec((1, seq, gw), lambda bi: (bi, 0, 0)),
                  pl.BlockSpec((1, n_mem, D_MODEL), lambda bi: (bi, 0, 0)),
                  const((1, D_MODEL)), const((D_MODEL, 2 * gw)), const((1, gw)), const((1, gw))],
        out_specs=pl.BlockSpec((1, seq, gw), lambda bi: (bi, 0, 0)),
        out_shape=jax.ShapeDtypeStruct((b, seq, gw), BF16),
        compiler_params=_params(("parallel",)),
        name="mem_attn",
    )(q, mem, gmem, wkv, gk, gout)


def _out_proj_kernel(x_ref, oa_ref, ob_ref, oc_ref, om_ref, w_ref, o_ref):
    acc = x_ref[...]
    for g, ref in enumerate((oa_ref, ob_ref, oc_ref, om_ref)):
        acc = acc + jnp.dot(ref[...], w_ref[g * GROUP_WIDTH:(g + 1) * GROUP_WIDTH, :],
                            preferred_element_type=F32)
    o_ref[...] = acc


def _out_proj(x2, oa, ob, oc, om, w):
    rows = x2.shape[0]
    tm = ROW_TILE
    gspec = pl.BlockSpec((tm, GROUP_WIDTH), lambda i: (i, 0))
    xspec = pl.BlockSpec((tm, D_MODEL), lambda i: (i, 0))
    return pl.pallas_call(
        _out_proj_kernel,
        grid=(rows // tm,),
        in_specs=[xspec, gspec, gspec, gspec, gspec, pl.BlockSpec((D_MODEL, D_MODEL), lambda i: (0, 0))],
        out_specs=xspec,
        out_shape=jax.ShapeDtypeStruct((rows, D_MODEL), F32),
        compiler_params=_params(("parallel",)),
        name="out_proj",
    )(x2, oa, ob, oc, om, w)


def _ffn_kernel(pos_blocks, x_ref, xh_ref, g_ref, wup_ref, cw_ref, cb_ref, wdown_ref, o_ref, h_ref, act_ref):
    tm = x_ref.shape[0]
    x = x_ref[...]
    keep = jnp.where(pl.program_id(0) % pos_blocks == 0, 0.0, 1.0)
    h_ref[0:HALO, :] = (_row_rms(xh_ref[...], g_ref[...]) * keep).astype(BF16)
    h_ref[HALO:, :] = _row_rms(x, g_ref[...]).astype(BF16)
    h = h_ref[...]

    def conv(col0):
        cs = slice(col0, col0 + FFN_CHUNK)
        u = jnp.dot(h, wup_ref[:, cs], preferred_element_type=F32)
        y = cb_ref[:, cs] + cw_ref[2:3, cs] * u[HALO:HALO + tm]
        y = y + cw_ref[1:2, cs] * u[HALO - 1:HALO - 1 + tm]
        return y + cw_ref[0:1, cs] * u[HALO - 2:HALO - 2 + tm]

    for c in range(D_FF // FFN_CHUNK):
        gate = conv(c * FFN_CHUNK)
        val = conv(D_FF + c * FFN_CHUNK)
        act = gate * (1.0 / (1.0 + jnp.exp(-gate))) * val
        act_ref[:, c * FFN_CHUNK:(c + 1) * FFN_CHUNK] = act.astype(BF16)
    o_ref[...] = x + jnp.dot(act_ref[...], wdown_ref[...], preferred_element_type=F32)


def _ffn(x2, g, wup, cw, cb, wdown, seq):
    rows = x2.shape[0]
    tm = ROW_TILE
    pos_blocks = seq // tm
    hb = tm // HALO
    const = lambda shape: pl.BlockSpec(shape, lambda i: (0, 0))
    xspec = pl.BlockSpec((tm, D_MODEL), lambda i: (i, 0))
    return pl.pallas_call(
        functools.partial(_ffn_kernel, pos_blocks),
        grid=(rows // tm,),
        in_specs=[xspec,
                  pl.BlockSpec((HALO, D_MODEL), lambda i: (jnp.maximum(i * hb - 1, 0), 0)),
                  const((1, D_MODEL)), const((D_MODEL, 2 * D_FF)), const((8, 2 * D_FF)),
                  const((1, 2 * D_FF)), const((D_FF, D_MODEL))],
        out_specs=xspec,
        out_shape=jax.ShapeDtypeStruct((rows, D_MODEL), F32),
        scratch_shapes=[pltpu.VMEM((HALO + tm, D_MODEL), BF16), pltpu.VMEM((tm, D_FF), BF16)],
        compiler_params=_params(("parallel",)),
        name="ffn",
    )(x2, x2, g, wup, cw, cb, wdown)


def _rope_tables(seq):
    inv = ROPE_THETA ** (-jnp.arange(0, HEAD_DIM, 2, dtype=F32) / HEAD_DIM)
    ang = jnp.arange(seq, dtype=F32)[:, None] * inv[None, :]
    ang = jnp.concatenate([ang, ang], axis=-1)
    sign = jnp.where(jnp.arange(HEAD_DIM) < HEAD_DIM // 2, -1.0, 1.0).astype(F32)
    cos = jnp.tile(jnp.cos(ang), (1, GROUP_HEADS))
    sin = jnp.tile(jnp.sin(ang) * sign[None, :], (1, GROUP_HEADS))
    return cos, sin


def _pad_rows(a, rows):
    return jnp.concatenate([a, jnp.zeros((rows - a.shape[0],) + a.shape[1:], a.dtype)], axis=0)


def kernel(x, mem, norm_mix, w_in, q_norm_a, k_norm_a, q_norm_b, k_norm_b, q_norm_m, k_norm_m,
           mem_norm, w_mem_kv, conv_w, conv_b, conv_ln_g, conv_ln_b, w_conv_out, out_norm, w_out,
           norm_ffn, w_up, ffn_conv_w, ffn_conv_b, w_down):
    b, seq, d = x.shape
    depth = w_in.shape[0]
    gw = GROUP_WIDTH
    cos, sin = _rope_tables(seq)
    x2 = x.reshape(b * seq, d)
    tile_h = lambda g: jnp.tile(g, GROUP_HEADS)
    r3 = lambda t: t.reshape(b, seq, gw)
    f2 = lambda t: t.reshape(b * seq, gw)
    for i in range(depth):
        gains = _pad_rows(jnp.stack([tile_h(q_norm_a[i]), tile_h(k_norm_a[i]), tile_h(q_norm_b[i]),
                                     tile_h(k_norm_b[i]), tile_h(q_norm_m[i])]), 8)
        w = w_in[i].astype(BF16)
        wvat = w[:, 2 * gw:3 * gw].T
        qa, ka, vat, qbc, kb, kbc, vb, vbc, u, qm = _in_proj(x2, norm_mix[i][None], w, wvat, cos, sin, gains, seq)
        gout = out_norm[i].reshape(4, 1, gw)
        oa = _moba(r3(qa), r3(ka), vat, gout[0])
        ob = _dilated(qbc, r3(kb), kbc, r3(vb), vbc, gout[1])
        oc = _conv_module(r3(u), _pad_rows(conv_w[i], CONV_PAD), conv_b[i][None], conv_ln_g[i][None],
                          conv_ln_b[i][None], w_conv_out[i].astype(BF16), gout[2])
        om = _mem_attn(r3(qm), mem, mem_norm[i][None], w_mem_kv[i].astype(BF16),
                       tile_h(k_norm_m[i])[None], gout[3])
        x2 = _out_proj(x2, f2(oa), f2(ob), f2(oc), f2(om), w_out[i].astype(BF16))
        x2 = _ffn(x2, norm_ffn[i][None], w_up[i].astype(BF16), _pad_rows(ffn_conv_w[i], 8),
                  ffn_conv_b[i][None], w_down[i].astype(BF16), seq)
    return x2.reshape(b, seq, d)
```

```python
import functools

import jax
import jax.numpy as jnp
from jax import lax
from jax.experimental import pallas as pl
from jax.experimental.pallas import tpu as pltpu

F32 = jnp.float32
BF16 = jnp.bfloat16

D_MODEL = 1024
HEAD_DIM = 64
GROUP_WIDTH = 256
GROUP_HEADS = 4
N_PROJ_GROUPS = 9
MOBA_BLOCK = 256
MOBA_TOPK = 3
BAND = 128
DIL_MID = 4
DIL_MAX = 16
CONV_WIDTH = 31
FFN_CONV_WIDTH = 3
D_FF = 2816
ROPE_THETA = 10000.0
EPS = 1e-6
ATTN_SCALE = HEAD_DIM ** -0.5
Q_SCALE = ATTN_SCALE * 1.4426950408889634
NEG = -1e30

ROW_TILE = 1024
FFN_CHUNK = 256
HALO = 16
PIPE_DEPTH = 3
PROJ_DEPTH = 2
STAT_ROWS = 8
VMEM_LIMIT = 56 * 1024 * 1024


def _params(sem):
    return pltpu.CompilerParams(dimension_semantics=sem, vmem_limit_bytes=VMEM_LIMIT)


def _lane_head(shape):
    return lax.broadcasted_iota(jnp.int32, shape, 1) // HEAD_DIM


def _head_ones():
    r = lax.broadcasted_iota(jnp.int32, (GROUP_WIDTH, GROUP_WIDTH), 0) // HEAD_DIM
    c = lax.broadcasted_iota(jnp.int32, (GROUP_WIDTH, GROUP_WIDTH), 1) // HEAD_DIM
    return jnp.where(r == c, 1.0, 0.0).astype(BF16)


def _head_rms(p, gain, ones_bd):
    ss = jnp.dot((p * p).astype(BF16), ones_bd, preferred_element_type=F32)
    return p * lax.rsqrt(ss * (1.0 / HEAD_DIM) + EPS) * gain


def _row_rms(x, gain):
    return x * lax.rsqrt(jnp.mean(x * x, axis=-1, keepdims=True) + EPS) * gain


def _dot_nt(a, b):
    return lax.dot_general(a, b, (((1,), (1,)), ((), ())), preferred_element_type=F32)


def _stack_heads(k):
    lh = _lane_head((1, GROUP_WIDTH))
    return jnp.concatenate([k * jnp.where(lh == h, 1.0, 0.0).astype(BF16) for h in range(GROUP_HEADS)], axis=0)


def _head_slice(h):
    return slice(h * HEAD_DIM, (h + 1) * HEAD_DIM)


def _transpose_bf16(v):
    return v.astype(F32).T.astype(BF16)


class _Pipe:
    def __init__(self, depth=3):
        self.depth = depth
        self.items = []

    def push(self, score, finish):
        self.items.append((score(), finish))
        if len(self.items) > self.depth:
            s, f = self.items.pop(0)
            f(s)

    def flush(self):
        for s, f in self.items:
            f(s)
        self.items = []


def _exp_pv(s, shift, vt_h):
    p = jnp.exp2(s - shift).astype(BF16)
    ones = jnp.ones((2 * STAT_ROWS, vt_h.shape[1]), BF16)
    pv = jnp.dot(jnp.concatenate([vt_h, ones], axis=0), p, preferred_element_type=F32)
    return pv[:HEAD_DIM], pv[HEAD_DIM:HEAD_DIM + 1]


def _softmax_pv(s, vt_h):
    m = jnp.max(s, axis=0, keepdims=True)
    o, l = _exp_pv(s, m, vt_h)
    return o, m, l


def _head_rows(parts):
    tq = parts[0][0].shape[1]
    rows = lambda xs: jnp.concatenate(xs, axis=0).T
    return (rows([o for o, _, _ in parts]),
            rows([jnp.broadcast_to(m, (HEAD_DIM, tq)) for _, m, _ in parts]),
            rows([jnp.broadcast_to(l, (HEAD_DIM, tq)) for _, _, l in parts]))


def _merge(acc0, m0, l0, acc1, m1, l1):
    mn = jnp.maximum(m0, m1)
    a0 = jnp.exp2(m0 - mn)
    a1 = jnp.exp2(m1 - mn)
    return acc0 * a0 + acc1 * a1, mn, l0 * a0 + l1 * a1


def _in_proj_kernel(x_ref, g_ref, w_ref, wvat_ref, cos_ref, sin_ref, gains_ref,
                    qa_ref, ka_ref, vat_ref, qbc_ref, kb_ref, kbc_ref, vb_ref, vbc_ref, u_ref, qm_ref,
                    perm_ref):
    h = _row_rms(x_ref[...], g_ref[...]).astype(BF16)
    ones_bd = _head_ones()
    cos = cos_ref[...]
    sin = sin_ref[...]
    tm = h.shape[0]
    lower = (lax.broadcasted_iota(jnp.int32, (tm, GROUP_WIDTH), 1) % HEAD_DIM) < (HEAD_DIM // 2)

    def proj(j):
        return jnp.dot(h, w_ref[:, j * GROUP_WIDTH:(j + 1) * GROUP_WIDTH], preferred_element_type=F32)

    def rope(y):
        rot = jnp.where(lower, pltpu.roll(y, GROUP_WIDTH - HEAD_DIM // 2, 1), pltpu.roll(y, HEAD_DIM // 2, 1))
        return y * cos + rot * sin

    def class_major(y, out_ref):
        for half in range(GROUP_WIDTH // BAND):
            ls = slice(half * BAND, (half + 1) * BAND)
            perm_ref[half] = y[:, ls]
            for r in range(DIL_MAX):
                out_ref[0, r, :, ls] = perm_ref[half, pl.ds(r, tm // DIL_MAX, stride=DIL_MAX), :].astype(BF16)

    def head_norm(y, gain_row):
        return _head_rms(y, gains_ref[gain_row:gain_row + 1, :], ones_bd)

    def fin_qa(y):
        qa_ref[...] = (rope(head_norm(y, 0)) * Q_SCALE).astype(BF16)

    def fin_ka(y):
        ka_ref[...] = rope(head_norm(y, 1)).astype(BF16)

    def fin_vat(y):
        vat = y.astype(BF16)
        for j in range(tm // MOBA_BLOCK):
            vat_ref[0, j] = vat[:, j * MOBA_BLOCK:(j + 1) * MOBA_BLOCK]

    def fin_qb(y):
        class_major(rope(head_norm(y, 2)) * Q_SCALE, qbc_ref)

    def fin_kb(y):
        kb = rope(head_norm(y, 3))
        kb_ref[...] = kb.astype(BF16)
        class_major(kb, kbc_ref)

    def fin_vb(y):
        vb_ref[...] = y.astype(BF16)
        class_major(y, vbc_ref)

    def fin_glu(ys):
        c_val, c_gate = ys
        u_ref[...] = (c_val * (1.0 / (1.0 + jnp.exp(-c_gate)))).astype(BF16)

    def fin_qm(y):
        qm_ref[...] = (head_norm(y, 4) * Q_SCALE).astype(BF16)

    pipe = _Pipe(PROJ_DEPTH)
    pipe.push(lambda: proj(0), fin_qa)
    pipe.push(lambda: proj(1), fin_ka)
    pipe.push(lambda: _dot_nt(wvat_ref[...], h), fin_vat)
    pipe.push(lambda: proj(3), fin_qb)
    pipe.push(lambda: proj(4), fin_kb)
    pipe.push(lambda: proj(5), fin_vb)
    pipe.push(lambda: (proj(6), proj(7)), fin_glu)
    pipe.push(lambda: proj(8), fin_qm)
    pipe.flush()


def _in_proj(x2, g, w, wvat, cos, sin, gains, seq):
    rows = x2.shape[0]
    b = rows // seq
    tm = ROW_TILE
    pos_blocks = seq // tm
    nat = jax.ShapeDtypeStruct((rows, GROUP_WIDTH), BF16)
    nat_spec = pl.BlockSpec((tm, GROUP_WIDTH), lambda i: (i, 0))
    cm = jax.ShapeDtypeStruct((b, DIL_MAX, seq // DIL_MAX, GROUP_WIDTH), BF16)
    cm_spec = pl.BlockSpec((1, DIL_MAX, tm // DIL_MAX, GROUP_WIDTH),
                           lambda i: (i // pos_blocks, 0, i % pos_blocks, 0))
    vat = jax.ShapeDtypeStruct((b, seq // MOBA_BLOCK, GROUP_WIDTH, MOBA_BLOCK), BF16)
    vat_spec = pl.BlockSpec((1, tm // MOBA_BLOCK, GROUP_WIDTH, MOBA_BLOCK),
                            lambda i: (i // pos_blocks, i % pos_blocks, 0, 0))
    return pl.pallas_call(
        _in_proj_kernel,
        grid=(rows // tm,),
        in_specs=[
            pl.BlockSpec((tm, D_MODEL), lambda i: (i, 0)),
            pl.BlockSpec((1, D_MODEL), lambda i: (0, 0)),
            pl.BlockSpec((D_MODEL, N_PROJ_GROUPS * GROUP_WIDTH), lambda i: (0, 0)),
            pl.BlockSpec((GROUP_WIDTH, D_MODEL), lambda i: (0, 0)),
            pl.BlockSpec((tm, GROUP_WIDTH), lambda i: (i % pos_blocks, 0)),
            pl.BlockSpec((tm, GROUP_WIDTH), lambda i: (i % pos_blocks, 0)),
            pl.BlockSpec((8, GROUP_WIDTH), lambda i: (0, 0)),
        ],
        out_specs=[nat_spec, nat_spec, vat_spec, cm_spec, nat_spec, cm_spec, nat_spec, cm_spec, nat_spec, nat_spec],
        out_shape=[nat, nat, vat, cm, nat, cm, nat, cm, nat, nat],
        scratch_shapes=[pltpu.VMEM((GROUP_WIDTH // BAND, tm, BAND), F32)],
        compiler_params=_params(("parallel",)),
        name="in_proj",
    )(x2, g, w, wvat, cos, sin, gains)


def _moba_kernel(q_ref, k_ref, vt_ref, gout_ref, o_ref):
    seq = k_ref.shape[1]
    nb = seq // MOBA_BLOCK
    tq = MOBA_BLOCK
    nrow = 2 * STAT_ROWS

    r = lax.broadcasted_iota(jnp.int32, (nrow, seq), 0)
    c = lax.broadcasted_iota(jnp.int32, (nrow, seq), 1) // MOBA_BLOCK
    avg = jnp.where(r == c, 1.0 / MOBA_BLOCK, 0.0).astype(BF16)
    kmean = jnp.dot(avg, k_ref[0], preferred_element_type=F32)
    khi = kmean.astype(BF16)
    klo = (kmean - khi.astype(F32)).astype(BF16)
    khi_st = _stack_heads(khi)
    klo_st = _stack_heads(klo)

    kk = lax.broadcasted_iota(jnp.int32, (tq, tq), 0)
    qq = lax.broadcasted_iota(jnp.int32, (tq, tq), 1)
    causal = kk <= qq
    blk = lax.broadcasted_iota(jnp.int32, (nrow, tq), 0)
    gout = gout_ref[...]
    pipe = _Pipe(PIPE_DEPTH)
    kst = {}

    def stacked(n):
        if n not in kst:
            kst[n] = _stack_heads(k_ref[0, n * tq:(n + 1) * tq, :])
        return kst[n]

    for i in range(nb):
        qs = slice(i * tq, (i + 1) * tq)
        q = q_ref[0, qs, :]
        state = [None] * GROUP_HEADS

        selb = [None] * GROUP_HEADS
        if i > MOBA_TOPK:
            past = blk < i
            g_all = _dot_nt(khi_st, q) + _dot_nt(klo_st, q)
            for h in range(GROUP_HEADS):
                g = jnp.where(past, g_all[h * nrow:(h + 1) * nrow], -jnp.inf)
                rank = jnp.zeros((nrow, tq), F32)
                for mth in range(i):
                    row = g[mth:mth + 1, :]
                    rank = rank + jnp.where(row == g, jnp.where(blk > mth, 1.0, 0.0),
                                            jnp.where(row > g, 1.0, 0.0))
                selb[h] = jnp.where(rank < MOBA_TOPK, 0.0, NEG)

        def finish_own(s_all, state=state, i=i):
            for h in range(GROUP_HEADS):
                s = jnp.where(causal, s_all[h * tq:(h + 1) * tq], NEG)
                state[h] = _softmax_pv(s, vt_ref[0, i, _head_slice(h), :])

        def finish_past(s_all, n, state=state, selb=selb):
            for h in range(GROUP_HEADS):
                s = s_all[h * tq:(h + 1) * tq]
                acc, m_old, l_old = state[h]
                smax = jnp.max(s, axis=0, keepdims=True)
                if selb[h] is None:
                    m_new = jnp.maximum(m_old, smax)
                    shift = m_new
                else:
                    bias = selb[h][n:n + 1, :]
                    m_new = jnp.maximum(m_old, smax + bias)
                    shift = m_new - bias
                alpha = jnp.exp2(m_old - m_new)
                pv, l_new = _exp_pv(s, shift, vt_ref[0, n, _head_slice(h), :])
                state[h] = (alpha * acc + pv, m_new, alpha * l_old + l_new)

        def finalize(state=state, qs=qs):
            ot = jnp.concatenate([acc / l for acc, _, l in state], axis=0)
            o_ref[0, qs, :] = _row_rms(ot.T, gout).astype(BF16)

        order = [i] + list(range(i))
        for pos, n in enumerate(order):
            score = lambda n=n, q=q: _dot_nt(stacked(n), q)
            if n == i:
                fin = finish_own
            else:
                fin = lambda s, n=n, f=finish_past: f(s, n)
            if pos == len(order) - 1:
                fin = lambda s, fin=fin, fz=finalize: (fin(s), fz())
            pipe.push(score, fin)
    pipe.flush()


def _moba(q, k, vt, gout):
    b, seq, gw = q.shape
    nb = seq // MOBA_BLOCK
    return pl.pallas_call(
        _moba_kernel,
        grid=(b,),
        in_specs=[pl.BlockSpec((1, seq, gw), lambda bi: (bi, 0, 0)),
                  pl.BlockSpec((1, seq, gw), lambda bi: (bi, 0, 0)),
                  pl.BlockSpec((1, nb, gw, MOBA_BLOCK), lambda bi: (bi, 0, 0, 0)),
                  pl.BlockSpec((1, gw), lambda bi: (0, 0))],
        out_specs=pl.BlockSpec((1, seq, gw), lambda bi: (bi, 0, 0)),
        out_shape=jax.ShapeDtypeStruct((b, seq, gw), BF16),
        compiler_params=_params(("parallel",)),
        name="moba",
    )(q, k, vt, gout)


def _dilated_kernel(qc_ref, k1_ref, kc_ref, v1_ref, vc_ref, b1_ref, b4_ref, b16_ref, gout_ref, o_ref,
                    acc_ref, m_ref, l_ref, v1t_ref, fin_ref):
    seq = k1_ref.shape[1]
    per = DIL_MAX // DIL_MID
    gout = gout_ref[...]
    pipe = _Pipe(PIPE_DEPTH)

    def tile(q, k, vt, bias, done):
        tk = k.shape[0]

        def finish(s_all):
            parts = [_softmax_pv(s_all[h * tk:(h + 1) * tk] + bias, vt[_head_slice(h), :])
                     for h in range(GROUP_HEADS)]
            done(*_head_rows(parts))

        pipe.push(lambda: _dot_nt(_stack_heads(k), q), finish)

    for c0 in range(0, seq, 2 * BAND):
        v1t_ref[:, c0:c0 + 2 * BAND] = _transpose_bf16(v1_ref[0, c0:c0 + 2 * BAND, :])

    for r in range(DIL_MAX):
        def done16(acc, m_all, l_all, r=r):
            acc_ref[r] = acc
            m_ref[r] = m_all
            l_ref[r] = l_all

        tile(qc_ref[0, r], kc_ref[0, r], _transpose_bf16(vc_ref[0, r]), b16_ref[...], done16)

    jq4 = BAND // per
    for a in range(DIL_MID):
        classes = [u * DIL_MID + a for u in range(per)]
        for jb in range(seq // DIL_MID // BAND):
            rs = slice(jb * jq4, (jb + 1) * jq4)
            if jb == 0:
                ks = rs
                bias = jnp.concatenate([b4_ref[2 * jq4 * u + jq4:2 * jq4 * (u + 1), :] for u in range(per)], axis=0)
            else:
                ks = slice((jb - 1) * jq4, (jb + 1) * jq4)
                bias = b4_ref[...]

            def done4(a1, m1, l1, classes=classes, rs=rs):
                for u, c in enumerate(classes):
                    ts = slice(u * jq4, (u + 1) * jq4)
                    acc, m_all, l_all = _merge(acc_ref[c, rs, :], m_ref[c, rs, :], l_ref[c, rs, :],
                                               a1[ts], m1[ts], l1[ts])
                    acc_ref[c, rs, :] = acc
                    m_ref[c, rs, :] = m_all
                    l_ref[c, rs, :] = l_all

            v = jnp.concatenate([vc_ref[0, c, ks, :] for c in classes], axis=0)
            tile(jnp.concatenate([qc_ref[0, c, rs, :] for c in classes], axis=0),
                 jnp.concatenate([kc_ref[0, c, ks, :] for c in classes], axis=0),
                 _transpose_bf16(v), bias, done4)

    tq = 2 * BAND
    jq1 = tq // DIL_MAX
    for jb in range(seq // tq):
        rs = slice(jb * jq1, (jb + 1) * jq1)
        if jb == 0:
            ks = slice(0, tq)
            bias = b1_ref[BAND:, :]
        else:
            ks = slice(jb * tq - BAND, (jb + 1) * tq)
            bias = b1_ref[...]

        def done1(a1, m1, l1, rs=rs, jb=jb):
            for r in range(DIL_MAX):
                ts = slice(r * jq1, (r + 1) * jq1)
                acc, _, l_all = _merge(acc_ref[r, rs, :], m_ref[r, rs, :], l_ref[r, rs, :],
                                       a1[ts], m1[ts], l1[ts])
                res = _row_rms(acc / l_all, gout)
                for half in range(GROUP_WIDTH // BAND):
                    fin_ref[half, pl.ds(r, jq1, stride=DIL_MAX), :] = res[:, half * BAND:(half + 1) * BAND]
            for half in range(GROUP_WIDTH // BAND):
                o_ref[0, jb * tq:(jb + 1) * tq, half * BAND:(half + 1) * BAND] = fin_ref[half].astype(BF16)

        tile(jnp.concatenate([qc_ref[0, r, rs, :] for r in range(DIL_MAX)], axis=0),
             k1_ref[0, ks, :], v1t_ref[:, ks], bias, done1)
    pipe.flush()


def _dilated_biases():
    def band(krel, qrel):
        dist = qrel[None, :] - krel[:, None]
        return jnp.where((dist >= 0) & (dist <= BAND), 0.0, NEG).astype(F32)

    tq = 2 * BAND
    jq = tq // DIL_MAX
    col = jnp.arange(tq)
    b1 = band(jnp.arange(BAND + tq) - BAND, DIL_MAX * (col % jq) + col // jq)
    per = DIL_MAX // DIL_MID
    jq = BAND // per
    col = jnp.arange(BAND)
    row = jnp.arange(2 * BAND)
    b4 = band(per * (row % (2 * jq) - jq) + row // (2 * jq), per * (col % jq) + col // jq)
    b16 = band(jnp.arange(BAND), jnp.arange(BAND))
    return b1, b4, b16


def _dilated(qc, k1, kc, v1, vc, gout):
    b, seq, gw = k1.shape
    b1, b4, b16 = _dilated_biases()
    nat = pl.BlockSpec((1, seq, gw), lambda bi: (bi, 0, 0))
    cm_shape = (DIL_MAX, seq // DIL_MAX, gw)
    cm = pl.BlockSpec((1,) + cm_shape, lambda bi: (bi, 0, 0, 0))
    const = lambda shape: pl.BlockSpec(shape, lambda bi: (0, 0))
    return pl.pallas_call(
        _dilated_kernel,
        grid=(b,),
        in_specs=[cm, nat, cm, nat, cm, const(b1.shape), const(b4.shape), const(b16.shape), const((1, gw))],
        out_specs=nat,
        out_shape=jax.ShapeDtypeStruct((b, seq, gw), BF16),
        scratch_shapes=[pltpu.VMEM(cm_shape, F32)] * 3 + [pltpu.VMEM((gw, seq), BF16),
                                                          pltpu.VMEM((gw // BAND, 2 * BAND, BAND), F32)],
        compiler_params=_params(("parallel",)),
        name="dilated",
    )(qc, k1, kc, v1, vc, b1, b4, b16, gout)


CONV_PAD = 32
CONV_ROWS = 256


def _conv_kernel(u_ref, w_ref, b_ref, lng_ref, lnb_ref, wout_ref, gout_ref, o_ref, pad_ref):
    seq = u_ref.shape[1]
    pad_ref[0:CONV_PAD, :] = jnp.zeros((CONV_PAD, GROUP_WIDTH), F32)
    pad_ref[CONV_PAD:CONV_PAD + seq, :] = u_ref[0].astype(F32)
    pad_ref[CONV_PAD + seq:, :] = jnp.zeros((16, GROUP_WIDTH), F32)
    lead = CONV_PAD - (CONV_WIDTH - 1)
    span = CONV_ROWS + 16
    for c0 in range(0, seq, CONV_ROWS):
        y = jnp.broadcast_to(b_ref[...], (CONV_ROWS, GROUP_WIDTH))
        for sub in range(8):
            z = None
            for a in range((CONV_WIDTH - sub + 7) // 8):
                j = 8 * a + sub
                t = w_ref[j:j + 1, :] * pad_ref[c0 + 8 * a:c0 + 8 * a + span, :]
                z = t if z is None else z + t
            y = y + z[lead + sub:lead + sub + CONV_ROWS]
        mu = jnp.mean(y, axis=-1, keepdims=True)
        d = y - mu
        var = jnp.mean(d * d, axis=-1, keepdims=True)
        yn = d * lax.rsqrt(var + EPS) * lng_ref[...] + lnb_ref[...]
        act = yn * (1.0 / (1.0 + jnp.exp(-yn)))
        oc = jnp.dot(act.astype(BF16), wout_ref[...], preferred_element_type=F32)
        o_ref[0, c0:c0 + CONV_ROWS, :] = _row_rms(oc, gout_ref[...]).astype(BF16)


def _conv_module(u, w, bias, lng, lnb, wout, gout):
    b, seq, gw = u.shape
    const = lambda shape: pl.BlockSpec(shape, lambda bi: (0, 0))
    return pl.pallas_call(
        _conv_kernel,
        grid=(b,),
        in_specs=[pl.BlockSpec((1, seq, gw), lambda bi: (bi, 0, 0)),
                  const((CONV_PAD, gw)), const((1, gw)), const((1, gw)), const((1, gw)),
                  const((gw, gw)), const((1, gw))],
        out_specs=pl.BlockSpec((1, seq, gw), lambda bi: (bi, 0, 0)),
        out_shape=jax.ShapeDtypeStruct((b, seq, gw), BF16),
        scratch_shapes=[pltpu.VMEM((CONV_PAD + seq + 16, gw), F32)],
        compiler_params=_params(("parallel",)),
        name="conv_module",
    )(u, w, bias, lng, lnb, wout, gout)


MEM_QROWS = 256


def _mem_kernel(q_ref, mem_ref, gmem_ref, wkv_ref, gk_ref, gout_ref, o_ref):
    seq = q_ref.shape[1]
    mh = _row_rms(mem_ref[0], gmem_ref[...]).astype(BF16)
    kv = jnp.dot(mh, wkv_ref[...], preferred_element_type=F32)
    km = _head_rms(kv[:, :GROUP_WIDTH], gk_ref[...], _head_ones()).astype(BF16)
    vmt = kv[:, GROUP_WIDTH:].T.astype(BF16)
    gout = gout_ref[...]
    pipe = _Pipe(PIPE_DEPTH)
    km_st = _stack_heads(km)
    n_mem = km.shape[0]
    for c0 in range(0, seq, MEM_QROWS):
        def finish(s_all, c0=c0):
            parts = [_softmax_pv(s_all[h * n_mem:(h + 1) * n_mem], vmt[_head_slice(h), :])
                     for h in range(GROUP_HEADS)]
            ot = jnp.concatenate([o / l for o, _, l in parts], axis=0)
            o_ref[0, c0:c0 + MEM_QROWS, :] = _row_rms(ot.T, gout).astype(BF16)

        pipe.push(lambda c0=c0: _dot_nt(km_st, q_ref[0, c0:c0 + MEM_QROWS, :]), finish)
    pipe.flush()


def _mem_attn(q, mem, gmem, wkv, gk, gout):
    b, seq, gw = q.shape
    n_mem = mem.shape[1]
    const = lambda shape: pl.BlockSpec(shape, lambda bi: (0, 0))
    return pl.pallas_call(
        _mem_kernel,
        grid=(b,),
        in_specs=[pl.BlockSpec((1, seq, gw), lambda bi: (bi, 0, 0)),
                  pl.BlockSpec((1, n_mem, D_MODEL), lambda bi: (bi, 0, 0)),
                  const((1, D_MODEL)), const((D_MODEL, 2 * gw)), const((1, gw)), const((1, gw))],
        out_specs=pl.BlockSpec((1, seq, gw), lambda bi: (bi, 0, 0)),
        out_shape=jax.ShapeDtypeStruct((b, seq, gw), BF16),
        compiler_params=_params(("parallel",)),
        name="mem_attn",
    )(q, mem, gmem, wkv, gk, gout)


def _ffn_kernel(pos_blocks, x_ref, xh_ref, oa_ref, ob_ref, oc_ref, om_ref, oah_ref, obh_ref, och_ref, omh_ref,
                wout_ref, g_ref, wup_ref, cw_ref, cb_ref, wdown_ref, o_ref, mix_ref, h_ref, act_ref):
    tm = x_ref.shape[0]
    groups = ((oa_ref, oah_ref), (ob_ref, obh_ref), (oc_ref, och_ref), (om_ref, omh_ref))
    for g, (main_ref, halo_ref) in enumerate(groups):
        cs = slice(g * GROUP_WIDTH, (g + 1) * GROUP_WIDTH)
        mix_ref[0:HALO, cs] = halo_ref[...]
        mix_ref[HALO:, cs] = main_ref[...]
    upd = jnp.dot(mix_ref[...], wout_ref[...], preferred_element_type=F32)
    x = x_ref[...] + upd[HALO:]
    xh = xh_ref[...] + upd[0:HALO]
    keep = jnp.where(pl.program_id(0) % pos_blocks == 0, 0.0, 1.0)
    h_ref[0:HALO, :] = (_row_rms(xh, g_ref[...]) * keep).astype(BF16)
    h_ref[HALO:, :] = _row_rms(x, g_ref[...]).astype(BF16)
    h = h_ref[...]

    def conv(col0):
        cs = slice(col0, col0 + FFN_CHUNK)
        u = jnp.dot(h, wup_ref[:, cs], preferred_element_type=F32)
        y = cb_ref[:, cs] + cw_ref[2:3, cs] * u[HALO:HALO + tm]
        y = y + cw_ref[1:2, cs] * u[HALO - 1:HALO - 1 + tm]
        return y + cw_ref[0:1, cs] * u[HALO - 2:HALO - 2 + tm]

    for c in range(D_FF // FFN_CHUNK):
        gate = conv(c * FFN_CHUNK)
        val = conv(D_FF + c * FFN_CHUNK)
        act = gate * (1.0 / (1.0 + jnp.exp(-gate))) * val
        act_ref[:, c * FFN_CHUNK:(c + 1) * FFN_CHUNK] = act.astype(BF16)
    o_ref[...] = x + jnp.dot(act_ref[...], wdown_ref[...], preferred_element_type=F32)


def _ffn(x2, mixes, wout, g, wup, cw, cb, wdown, seq):
    rows = x2.shape[0]
    tm = ROW_TILE
    pos_blocks = seq // tm
    hb = tm // HALO
    const = lambda shape: pl.BlockSpec(shape, lambda i: (0, 0), pipeline_mode=pl.Buffered(1))
    halo = lambda i: (jnp.maximum(i * hb - 1, 0), 0)
    xspec = pl.BlockSpec((tm, D_MODEL), lambda i: (i, 0))
    return pl.pallas_call(
        functools.partial(_ffn_kernel, pos_blocks),
        grid=(rows // tm,),
        in_specs=[xspec, pl.BlockSpec((HALO, D_MODEL), halo)]
                 + [pl.BlockSpec((tm, GROUP_WIDTH), lambda i: (i, 0))] * len(mixes)
                 + [pl.BlockSpec((HALO, GROUP_WIDTH), halo)] * len(mixes)
                 + [const((D_MODEL, D_MODEL)), const((1, D_MODEL)), const((D_MODEL, 2 * D_FF)),
                    const((8, 2 * D_FF)), const((1, 2 * D_FF)), const((D_FF, D_MODEL))],
        out_specs=xspec,
        out_shape=jax.ShapeDtypeStruct((rows, D_MODEL), F32),
        scratch_shapes=[pltpu.VMEM((HALO + tm, D_MODEL), BF16), pltpu.VMEM((HALO + tm, D_MODEL), BF16),
                        pltpu.VMEM((tm, D_FF), BF16)],
        compiler_params=_params(("parallel",)),
        name="ffn",
    )(x2, x2, *mixes, *mixes, wout, g, wup, cw, cb, wdown)


def _rope_tables(seq):
    inv = ROPE_THETA ** (-jnp.arange(0, HEAD_DIM, 2, dtype=F32) / HEAD_DIM)
    ang = jnp.arange(seq, dtype=F32)[:, None] * inv[None, :]
    ang = jnp.concatenate([ang, ang], axis=-1)
    sign = jnp.where(jnp.arange(HEAD_DIM) < HEAD_DIM // 2, -1.0, 1.0).astype(F32)
    cos = jnp.tile(jnp.cos(ang), (1, GROUP_HEADS))
    sin = jnp.tile(jnp.sin(ang) * sign[None, :], (1, GROUP_HEADS))
    return cos, sin


def _pad_rows(a, rows):
    return jnp.concatenate([a, jnp.zeros((rows - a.shape[0],) + a.shape[1:], a.dtype)], axis=0)


def kernel(x, mem, norm_mix, w_in, q_norm_a, k_norm_a, q_norm_b, k_norm_b, q_norm_m, k_norm_m,
           mem_norm, w_mem_kv, conv_w, conv_b, conv_ln_g, conv_ln_b, w_conv_out, out_norm, w_out,
           norm_ffn, w_up, ffn_conv_w, ffn_conv_b, w_down):
    b, seq, d = x.shape
    depth = w_in.shape[0]
    gw = GROUP_WIDTH
    cos, sin = _rope_tables(seq)
    x2 = x.reshape(b * seq, d)
    tile_h = lambda g: jnp.tile(g, GROUP_HEADS)
    r3 = lambda t: t.reshape(b, seq, gw)
    f2 = lambda t: t.reshape(b * seq, gw)
    for i in range(depth):
        gains = _pad_rows(jnp.stack([tile_h(q_norm_a[i]), tile_h(k_norm_a[i]), tile_h(q_norm_b[i]),
                                     tile_h(k_norm_b[i]), tile_h(q_norm_m[i])]), 8)
        w = w_in[i].astype(BF16)
        wvat = w[:, 2 * gw:3 * gw].T
        qa, ka, vat, qbc, kb, kbc, vb, vbc, u, qm = _in_proj(x2, norm_mix[i][None], w, wvat, cos, sin, gains, seq)
        gout = out_norm[i].reshape(4, 1, gw)
        oa = _moba(r3(qa), r3(ka), vat, gout[0])
        ob = _dilated(qbc, r3(kb), kbc, r3(vb), vbc, gout[1])
        oc = _conv_module(r3(u), _pad_rows(conv_w[i], CONV_PAD), conv_b[i][None], conv_ln_g[i][None],
                          conv_ln_b[i][None], w_conv_out[i].astype(BF16), gout[2])
        om = _mem_attn(r3(qm), mem, mem_norm[i][None], w_mem_kv[i].astype(BF16),
                       tile_h(k_norm_m[i])[None], gout[3])
        x2 = _ffn(x2, [f2(oa), f2(ob), f2(oc), f2(om)], w_out[i].astype(BF16),
                  norm_ffn[i][None], w_up[i].astype(BF16), _pad_rows(ffn_conv_w[i], 8),
                  ffn_conv_b[i][None], w_down[i].astype(BF16), seq)
    return x2.reshape(b, seq, d)
```

```python
import functools

import jax
import jax.numpy as jnp
from jax import lax
from jax.experimental import pallas as pl
from jax.experimental.pallas import tpu as pltpu

F32 = jnp.float32
BF16 = jnp.bfloat16

D_MODEL = 1024
HEAD_DIM = 64
GROUP_WIDTH = 256
GROUP_HEADS = 4
N_PROJ_GROUPS = 9
MOBA_BLOCK = 256
MOBA_TOPK = 3
BAND = 128
DIL_MID = 4
DIL_MAX = 16
CONV_WIDTH = 31
FFN_CONV_WIDTH = 3
D_FF = 2816
ROPE_THETA = 10000.0
EPS = 1e-6
ATTN_SCALE = HEAD_DIM ** -0.5
Q_SCALE = ATTN_SCALE * 1.4426950408889634
NEG = -1e30

ROW_TILE = 1024
FFN_CHUNK = 256
HALO = 16
CONV_PAD = 32
CONV_ROWS = 512
PIPE_DEPTH = 3
PROJ_DEPTH = 1
STAT_ROWS = 8
VMEM_LIMIT = 56 * 1024 * 1024


def _params(sem):
    return pltpu.CompilerParams(dimension_semantics=sem, vmem_limit_bytes=VMEM_LIMIT)


def _lane_head(shape):
    return lax.broadcasted_iota(jnp.int32, shape, 1) // HEAD_DIM


def _head_ones():
    r = lax.broadcasted_iota(jnp.int32, (GROUP_WIDTH, GROUP_WIDTH), 0) // HEAD_DIM
    c = lax.broadcasted_iota(jnp.int32, (GROUP_WIDTH, GROUP_WIDTH), 1) // HEAD_DIM
    return jnp.where(r == c, 1.0, 0.0).astype(BF16)


def _head_rms(p, gain, ones_bd):
    ss = jnp.dot((p * p).astype(BF16), ones_bd, preferred_element_type=F32)
    return p * lax.rsqrt(ss * (1.0 / HEAD_DIM) + EPS) * gain


def _row_rms(x, gain):
    return x * lax.rsqrt(jnp.mean(x * x, axis=-1, keepdims=True) + EPS) * gain


def _dot_nt(a, b):
    return lax.dot_general(a, b, (((1,), (1,)), ((), ())), preferred_element_type=F32)


def _stack_heads(k):
    lh = _lane_head((1, GROUP_WIDTH))
    return jnp.concatenate([k * jnp.where(lh == h, 1.0, 0.0).astype(BF16) for h in range(GROUP_HEADS)], axis=0)


def _head_slice(h):
    return slice(h * HEAD_DIM, (h + 1) * HEAD_DIM)


def _transpose_bf16(v):
    return v.astype(F32).T.astype(BF16)


class _Pipe:
    def __init__(self, depth=3):
        self.depth = depth
        self.items = []

    def push(self, score, finish):
        self.items.append((score(), finish))
        if len(self.items) > self.depth:
            s, f = self.items.pop(0)
            f(s)

    def flush(self):
        for s, f in self.items:
            f(s)
        self.items = []


def _scores(k_st, q):
    return _dot_nt(k_st, q).astype(BF16)


def _col_max(s):
    tk, tq = s.shape
    part = jnp.max(s.reshape(tk // 16, 16, tq), axis=0)
    return jnp.max(part.astype(F32), axis=0, keepdims=True)


def _exp_pv(s, shift, vt_h):
    p = jnp.exp2(s - shift.astype(BF16))
    ones = jnp.ones((2 * STAT_ROWS, vt_h.shape[1]), BF16)
    pv = jnp.dot(jnp.concatenate([vt_h, ones], axis=0), p, preferred_element_type=F32)
    return pv[:HEAD_DIM], pv[HEAD_DIM:HEAD_DIM + 1]


def _softmax_pv(s, vt_h):
    m = _col_max(s)
    o, l = _exp_pv(s, m, vt_h)
    return o, m, l


def _head_rows(parts):
    tq = parts[0][0].shape[1]
    rows = lambda xs: jnp.concatenate(xs, axis=0).T
    return (rows([o for o, _, _ in parts]),
            rows([jnp.broadcast_to(m, (HEAD_DIM, tq)) for _, m, _ in parts]),
            rows([jnp.broadcast_to(l, (HEAD_DIM, tq)) for _, _, l in parts]))


def _merge(acc0, m0, l0, acc1, m1, l1):
    mn = jnp.maximum(m0, m1)
    a0 = jnp.exp2(m0 - mn)
    a1 = jnp.exp2(m1 - mn)
    return acc0 * a0 + acc1 * a1, mn, l0 * a0 + l1 * a1


def _in_proj_kernel(x_ref, g_ref, w_ref, wvat_ref, cos_ref, sin_ref, gains_ref,
                    qa_ref, ka_ref, vat_ref, qbc_ref, kb_ref, kbc_ref, vb_ref, vbc_ref, u_ref, qm_ref,
                    perm_ref):
    h = _row_rms(x_ref[...], g_ref[...]).astype(BF16)
    ones_bd = _head_ones()
    cos = cos_ref[...]
    sin = sin_ref[...]
    tm = h.shape[0]
    lower = (lax.broadcasted_iota(jnp.int32, (tm, GROUP_WIDTH), 1) % HEAD_DIM) < (HEAD_DIM // 2)

    def proj(j):
        return jnp.dot(h, w_ref[:, j * GROUP_WIDTH:(j + 1) * GROUP_WIDTH], preferred_element_type=F32)

    def rope(y):
        rot = jnp.where(lower, pltpu.roll(y, GROUP_WIDTH - HEAD_DIM // 2, 1), pltpu.roll(y, HEAD_DIM // 2, 1))
        return y * cos + rot * sin

    def class_major(y, out_ref):
        for half in range(GROUP_WIDTH // BAND):
            ls = slice(half * BAND, (half + 1) * BAND)
            perm_ref[half] = y[:, ls]
            for r in range(DIL_MAX):
                out_ref[0, r, :, ls] = perm_ref[half, pl.ds(r, tm // DIL_MAX, stride=DIL_MAX), :].astype(BF16)

    def head_norm(y, gain_row):
        return _head_rms(y, gains_ref[gain_row:gain_row + 1, :], ones_bd)

    def fin_qa(y):
        qa_ref[...] = (rope(head_norm(y, 0)) * Q_SCALE).astype(BF16)

    def fin_ka(y):
        ka_ref[...] = rope(head_norm(y, 1)).astype(BF16)

    def fin_vat(y):
        vat = y.astype(BF16)
        for j in range(tm // MOBA_BLOCK):
            vat_ref[0, j] = vat[:, j * MOBA_BLOCK:(j + 1) * MOBA_BLOCK]

    def fin_qb(y):
        class_major(rope(head_norm(y, 2)) * Q_SCALE, qbc_ref)

    def fin_kb(y):
        kb = rope(head_norm(y, 3))
        kb_ref[...] = kb.astype(BF16)
        class_major(kb, kbc_ref)

    def fin_vb(y):
        vb_ref[...] = y.astype(BF16)
        class_major(y, vbc_ref)

    def fin_glu(ys):
        c_val, c_gate = ys
        u_ref[...] = (c_val * (1.0 / (1.0 + jnp.exp(-c_gate)))).astype(BF16)

    def fin_qm(y):
        qm_ref[...] = (head_norm(y, 4) * Q_SCALE).astype(BF16)

    pipe = _Pipe(PROJ_DEPTH)
    pipe.push(lambda: proj(0), fin_qa)
    pipe.push(lambda: proj(1), fin_ka)
    pipe.push(lambda: _dot_nt(wvat_ref[...], h), fin_vat)
    pipe.push(lambda: proj(3), fin_qb)
    pipe.push(lambda: proj(4), fin_kb)
    pipe.push(lambda: proj(5), fin_vb)
    pipe.push(lambda: (proj(6), proj(7)), fin_glu)
    pipe.push(lambda: proj(8), fin_qm)
    pipe.flush()


def _layer_spec(layer, shape):
    return pl.BlockSpec((None,) + shape, lambda *_: (layer, 0, 0))


def _in_proj(x2, g, w, layer, wvat, cos, sin, gains, seq):
    rows = x2.shape[0]
    b = rows // seq
    tm = ROW_TILE
    pos_blocks = seq // tm
    nat = jax.ShapeDtypeStruct((rows, GROUP_WIDTH), BF16)
    nat_spec = pl.BlockSpec((tm, GROUP_WIDTH), lambda i: (i, 0))
    cm = jax.ShapeDtypeStruct((b, DIL_MAX, seq // DIL_MAX, GROUP_WIDTH), BF16)
    cm_spec = pl.BlockSpec((1, DIL_MAX, tm // DIL_MAX, GROUP_WIDTH),
                           lambda i: (i // pos_blocks, 0, i % pos_blocks, 0))
    vat = jax.ShapeDtypeStruct((b, seq // MOBA_BLOCK, GROUP_WIDTH, MOBA_BLOCK), BF16)
    vat_spec = pl.BlockSpec((1, tm // MOBA_BLOCK, GROUP_WIDTH, MOBA_BLOCK),
                            lambda i: (i // pos_blocks, i % pos_blocks, 0, 0))
    return pl.pallas_call(
        _in_proj_kernel,
        grid=(rows // tm,),
        in_specs=[
            pl.BlockSpec((tm, D_MODEL), lambda i: (i, 0)),
            pl.BlockSpec((1, D_MODEL), lambda i: (0, 0)),
            _layer_spec(layer, (D_MODEL, N_PROJ_GROUPS * GROUP_WIDTH)),
            pl.BlockSpec((GROUP_WIDTH, D_MODEL), lambda i: (0, 0)),
            pl.BlockSpec((tm, GROUP_WIDTH), lambda i: (i % pos_blocks, 0)),
            pl.BlockSpec((tm, GROUP_WIDTH), lambda i: (i % pos_blocks, 0)),
            pl.BlockSpec((8, GROUP_WIDTH), lambda i: (0, 0)),
        ],
        out_specs=[nat_spec, nat_spec, vat_spec, cm_spec, nat_spec, cm_spec, nat_spec, cm_spec, nat_spec, nat_spec],
        out_shape=[nat, nat, vat, cm, nat, cm, nat, cm, nat, nat],
        scratch_shapes=[pltpu.VMEM((GROUP_WIDTH // BAND, tm, BAND), F32)],
        compiler_params=_params(("parallel",)),
        name="in_proj",
    )(x2, g, w, wvat, cos, sin, gains)


def _moba_kernel(q_ref, k_ref, vt_ref, gout_ref, o_ref):
    seq = k_ref.shape[1]
    nb = seq // MOBA_BLOCK
    tq = MOBA_BLOCK
    nrow = 2 * STAT_ROWS

    r = lax.broadcasted_iota(jnp.int32, (nrow, seq), 0)
    c = lax.broadcasted_iota(jnp.int32, (nrow, seq), 1) // MOBA_BLOCK
    avg = jnp.where(r == c, 1.0 / MOBA_BLOCK, 0.0).astype(BF16)
    kmean = jnp.dot(avg, k_ref[0], preferred_element_type=F32)
    khi = kmean.astype(BF16)
    klo = (kmean - khi.astype(F32)).astype(BF16)
    khi_st = _stack_heads(khi)
    klo_st = _stack_heads(klo)

    kk = lax.broadcasted_iota(jnp.int32, (tq, tq), 0)
    qq = lax.broadcasted_iota(jnp.int32, (tq, tq), 1)
    causal = jnp.where(kk <= qq, 0.0, NEG).astype(BF16)
    blk = lax.broadcasted_iota(jnp.int32, (nrow, tq), 0)
    gout = gout_ref[...]
    pipe = _Pipe(PIPE_DEPTH)
    kst = {}

    def stacked(n):
        if n not in kst:
            kst[n] = _stack_heads(k_ref[0, n * tq:(n + 1) * tq, :])
        return kst[n]

    for i in range(nb):
        qs = slice(i * tq, (i + 1) * tq)
        q = q_ref[0, qs, :]
        state = [None] * GROUP_HEADS

        selb = [None] * GROUP_HEADS
        if i > MOBA_TOPK:
            past = blk < i
            g_all = _dot_nt(khi_st, q) + _dot_nt(klo_st, q)
            for h in range(GROUP_HEADS):
                g = jnp.where(past, g_all[h * nrow:(h + 1) * nrow], -jnp.inf)
                rank = jnp.zeros((nrow, tq), F32)
                for mth in range(i):
                    row = g[mth:mth + 1, :]
                    rank = rank + jnp.where(row == g, jnp.where(blk > mth, 1.0, 0.0),
                                            jnp.where(row > g, 1.0, 0.0))
                selb[h] = jnp.where(rank < MOBA_TOPK, 0.0, NEG)

        def finish_own(s_all, state=state, i=i):
            for h in range(GROUP_HEADS):
                s = s_all[h * tq:(h + 1) * tq] + causal
                state[h] = _softmax_pv(s, vt_ref[0, i, _head_slice(h), :])

        def finish_past(s_all, n, state=state, selb=selb):
            for h in range(GROUP_HEADS):
                s = s_all[h * tq:(h + 1) * tq]
                acc, m_old, l_old = state[h]
                smax = _col_max(s)
                if selb[h] is None:
                    m_new = jnp.maximum(m_old, smax)
                    shift = m_new
                else:
                    bias = selb[h][n:n + 1, :]
                    m_new = jnp.maximum(m_old, smax + bias)
                    shift = m_new - bias
                alpha = jnp.exp2(m_old - m_new)
                pv, l_new = _exp_pv(s, shift, vt_ref[0, n, _head_slice(h), :])
                state[h] = (alpha * acc + pv, m_new, alpha * l_old + l_new)

        def finalize(state=state, qs=qs):
            ot = jnp.concatenate([acc / l for acc, _, l in state], axis=0)
            o_ref[0, qs, :] = _row_rms(ot.T, gout).astype(BF16)

        order = [i] + list(range(i))
        for pos, n in enumerate(order):
            score = lambda n=n, q=q: _scores(stacked(n), q)
            if n == i:
                fin = finish_own
            else:
                fin = lambda s, n=n, f=finish_past: f(s, n)
            if pos == len(order) - 1:
                fin = lambda s, fin=fin, fz=finalize: (fin(s), fz())
            pipe.push(score, fin)
    pipe.flush()


def _moba(q, k, vt, gout):
    b, seq, gw = q.shape
    nb = seq // MOBA_BLOCK
    return pl.pallas_call(
        _moba_kernel,
        grid=(b,),
        in_specs=[pl.BlockSpec((1, seq, gw), lambda bi: (bi, 0, 0)),
                  pl.BlockSpec((1, seq, gw), lambda bi: (bi, 0, 0)),
                  pl.BlockSpec((1, nb, gw, MOBA_BLOCK), lambda bi: (bi, 0, 0, 0)),
                  pl.BlockSpec((1, gw), lambda bi: (0, 0))],
        out_specs=pl.BlockSpec((1, seq, gw), lambda bi: (bi, 0, 0)),
        out_shape=jax.ShapeDtypeStruct((b, seq, gw), BF16),
        compiler_params=_params(("parallel",)),
        name="moba",
    )(q, k, vt, gout)


def _dilated_kernel(qc_ref, k1_ref, kc_ref, v1_ref, vc_ref, b1_ref, b4_ref, b16_ref, gout_ref, o_ref,
                    acc_ref, m_ref, l_ref, v1t_ref, fin_ref):
    seq = k1_ref.shape[1]
    per = DIL_MAX // DIL_MID
    gout = gout_ref[...]
    pipe = _Pipe(PIPE_DEPTH)

    def tile(q, k, vt, bias, done):
        tk = k.shape[0]

        def finish(s_all):
            parts = [_softmax_pv(s_all[h * tk:(h + 1) * tk] + bias, vt[_head_slice(h), :])
                     for h in range(GROUP_HEADS)]
            done(*_head_rows(parts))

        pipe.push(lambda: _scores(_stack_heads(k), q), finish)

    for c0 in range(0, seq, 2 * BAND):
        v1t_ref[:, c0:c0 + 2 * BAND] = _transpose_bf16(v1_ref[0, c0:c0 + 2 * BAND, :])

    for r in range(DIL_MAX):
        def done16(acc, m_all, l_all, r=r):
            acc_ref[r] = acc
            m_ref[r] = m_all
            l_ref[r] = l_all

        tile(qc_ref[0, r], kc_ref[0, r], _transpose_bf16(vc_ref[0, r]), b16_ref[...], done16)

    jq4 = BAND // per
    for a in range(DIL_MID):
        classes = [u * DIL_MID + a for u in range(per)]
        for jb in range(seq // DIL_MID // BAND):
            rs = slice(jb * jq4, (jb + 1) * jq4)
            if jb == 0:
                ks = rs
                bias = jnp.concatenate([b4_ref[2 * jq4 * u + jq4:2 * jq4 * (u + 1), :] for u in range(per)], axis=0)
            else:
                ks = slice((jb - 1) * jq4, (jb + 1) * jq4)
                bias = b4_ref[...]

            def done4(a1, m1, l1, classes=classes, rs=rs):
                for u, c in enumerate(classes):
                    ts = slice(u * jq4, (u + 1) * jq4)
                    acc, m_all, l_all = _merge(acc_ref[c, rs, :], m_ref[c, rs, :], l_ref[c, rs, :],
                                               a1[ts], m1[ts], l1[ts])
                    acc_ref[c, rs, :] = acc
                    m_ref[c, rs, :] = m_all
                    l_ref[c, rs, :] = l_all

            v = jnp.concatenate([vc_ref[0, c, ks, :] for c in classes], axis=0)
            tile(jnp.concatenate([qc_ref[0, c, rs, :] for c in classes], axis=0),
                 jnp.concatenate([kc_ref[0, c, ks, :] for c in classes], axis=0),
                 _transpose_bf16(v), bias, done4)

    tq = 2 * BAND
    jq1 = tq // DIL_MAX
    for jb in range(seq // tq):
        rs = slice(jb * jq1, (jb + 1) * jq1)
        if jb == 0:
            ks = slice(0, tq)
            bias = b1_ref[BAND:, :]
        else:
            ks = slice(jb * tq - BAND, (jb + 1) * tq)
            bias = b1_ref[...]

        def done1(a1, m1, l1, rs=rs, jb=jb):
            for r in range(DIL_MAX):
                ts = slice(r * jq1, (r + 1) * jq1)
                acc, _, l_all = _merge(acc_ref[r, rs, :], m_ref[r, rs, :], l_ref[r, rs, :],
                                       a1[ts], m1[ts], l1[ts])
                res = _row_rms(acc / l_all, gout)
                for half in range(GROUP_WIDTH // BAND):
                    fin_ref[half, pl.ds(r, jq1, stride=DIL_MAX), :] = res[:, half * BAND:(half + 1) * BAND]
            for half in range(GROUP_WIDTH // BAND):
                o_ref[0, jb * tq:(jb + 1) * tq, half * BAND:(half + 1) * BAND] = fin_ref[half].astype(BF16)

        tile(jnp.concatenate([qc_ref[0, r, rs, :] for r in range(DIL_MAX)], axis=0),
             k1_ref[0, ks, :], v1t_ref[:, ks], bias, done1)
    pipe.flush()


def _dilated_biases():
    def band(krel, qrel):
        dist = qrel[None, :] - krel[:, None]
        return jnp.where((dist >= 0) & (dist <= BAND), 0.0, NEG).astype(BF16)

    tq = 2 * BAND
    jq = tq // DIL_MAX
    col = jnp.arange(tq)
    b1 = band(jnp.arange(BAND + tq) - BAND, DIL_MAX * (col % jq) + col // jq)
    per = DIL_MAX // DIL_MID
    jq = BAND // per
    col = jnp.arange(BAND)
    row = jnp.arange(2 * BAND)
    b4 = band(per * (row % (2 * jq) - jq) + row // (2 * jq), per * (col % jq) + col // jq)
    b16 = band(jnp.arange(BAND), jnp.arange(BAND))
    return b1, b4, b16


def _dilated(qc, k1, kc, v1, vc, gout):
    b, seq, gw = k1.shape
    b1, b4, b16 = _dilated_biases()
    nat = pl.BlockSpec((1, seq, gw), lambda bi: (bi, 0, 0))
    cm_shape = (DIL_MAX, seq // DIL_MAX, gw)
    cm = pl.BlockSpec((1,) + cm_shape, lambda bi: (bi, 0, 0, 0))
    const = lambda shape: pl.BlockSpec(shape, lambda bi: (0, 0))
    return pl.pallas_call(
        _dilated_kernel,
        grid=(b,),
        in_specs=[cm, nat, cm, nat, cm, const(b1.shape), const(b4.shape), const(b16.shape), const((1, gw))],
        out_specs=nat,
        out_shape=jax.ShapeDtypeStruct((b, seq, gw), BF16),
        scratch_shapes=[pltpu.VMEM(cm_shape, F32)] * 3 + [pltpu.VMEM((gw, seq), BF16),
                                                          pltpu.VMEM((gw // BAND, 2 * BAND, BAND), F32)],
        compiler_params=_params(("parallel",)),
        name="dilated",
    )(qc, k1, kc, v1, vc, b1, b4, b16, gout)


def _conv_kernel(u_ref, w_ref, b_ref, lng_ref, lnb_ref, wout_ref, gout_ref, o_ref, pad_ref):
    seq = u_ref.shape[1]
    pad_ref[0:CONV_PAD, :] = jnp.zeros((CONV_PAD, GROUP_WIDTH), F32)
    pad_ref[CONV_PAD:CONV_PAD + seq, :] = u_ref[0].astype(F32)
    pad_ref[CONV_PAD + seq:, :] = jnp.zeros((16, GROUP_WIDTH), F32)
    lead = CONV_PAD - (CONV_WIDTH - 1)
    span = CONV_ROWS + 16
    for c0 in range(0, seq, CONV_ROWS):
        y = jnp.broadcast_to(b_ref[...], (CONV_ROWS, GROUP_WIDTH))
        for sub in range(8):
            z = None
            for a in range((CONV_WIDTH - sub + 7) // 8):
                j = 8 * a + sub
                t = w_ref[j:j + 1, :] * pad_ref[c0 + 8 * a:c0 + 8 * a + span, :]
                z = t if z is None else z + t
            y = y + z[lead + sub:lead + sub + CONV_ROWS]
        mu = jnp.mean(y, axis=-1, keepdims=True)
        d = y - mu
        var = jnp.mean(d * d, axis=-1, keepdims=True)
        yn = d * lax.rsqrt(var + EPS) * lng_ref[...] + lnb_ref[...]
        act = yn * (1.0 / (1.0 + jnp.exp(-yn)))
        oc = jnp.dot(act.astype(BF16), wout_ref[...], preferred_element_type=F32)
        o_ref[0, c0:c0 + CONV_ROWS, :] = _row_rms(oc, gout_ref[...]).astype(BF16)


def _conv_module(u, w, bias, lng, lnb, wout, layer, gout):
    b, seq, gw = u.shape
    const = lambda shape: pl.BlockSpec(shape, lambda bi: (0, 0))
    return pl.pallas_call(
        _conv_kernel,
        grid=(b,),
        in_specs=[pl.BlockSpec((1, seq, gw), lambda bi: (bi, 0, 0)),
                  const((CONV_PAD, gw)), const((1, gw)), const((1, gw)), const((1, gw)),
                  _layer_spec(layer, (gw, gw)), const((1, gw))],
        out_specs=pl.BlockSpec((1, seq, gw), lambda bi: (bi, 0, 0)),
        out_shape=jax.ShapeDtypeStruct((b, seq, gw), BF16),
        scratch_shapes=[pltpu.VMEM((CONV_PAD + seq + 16, gw), F32)],
        compiler_params=_params(("parallel",)),
        name="conv_module",
    )(u, w, bias, lng, lnb, wout, gout)


MEM_QROWS = 256


def _mem_kernel(q_ref, mem_ref, gmem_ref, wkv_ref, gk_ref, gout_ref, o_ref):
    seq = q_ref.shape[1]
    mh = _row_rms(mem_ref[0], gmem_ref[...]).astype(BF16)
    kv = jnp.dot(mh, wkv_ref[...], preferred_element_type=F32)
    km = _head_rms(kv[:, :GROUP_WIDTH], gk_ref[...], _head_ones()).astype(BF16)
    vmt = kv[:, GROUP_WIDTH:].T.astype(BF16)
    gout = gout_ref[...]
    pipe = _Pipe(PIPE_DEPTH)
    km_st = _stack_heads(km)
    n_mem = km.shape[0]
    for c0 in range(0, seq, MEM_QROWS):
        def finish(s_all, c0=c0):
            parts = [_softmax_pv(s_all[h * n_mem:(h + 1) * n_mem], vmt[_head_slice(h), :])
                     for h in range(GROUP_HEADS)]
            ot = jnp.concatenate([o / l for o, _, l in parts], axis=0)
            o_ref[0, c0:c0 + MEM_QROWS, :] = _row_rms(ot.T, gout).astype(BF16)

        pipe.push(lambda c0=c0: _scores(km_st, q_ref[0, c0:c0 + MEM_QROWS, :]), finish)
    pipe.flush()


def _mem_attn(q, mem, gmem, wkv, layer, gk, gout):
    b, seq, gw = q.shape
    n_mem = mem.shape[1]
    const = lambda shape: pl.BlockSpec(shape, lambda bi: (0, 0))
    return pl.pallas_call(
        _mem_kernel,
        grid=(b,),
        in_specs=[pl.BlockSpec((1, seq, gw), lambda bi: (bi, 0, 0)),
                  pl.BlockSpec((1, n_mem, D_MODEL), lambda bi: (bi, 0, 0)),
                  const((1, D_MODEL)), _layer_spec(layer, (D_MODEL, 2 * gw)), const((1, gw)), const((1, gw))],
        out_specs=pl.BlockSpec((1, seq, gw), lambda bi: (bi, 0, 0)),
        out_shape=jax.ShapeDtypeStruct((b, seq, gw), BF16),
        compiler_params=_params(("parallel",)),
        name="mem_attn",
    )(q, mem, gmem, wkv, gk, gout)


def _ffn_kernel(pos_blocks, x_ref, xh_ref, oa_ref, ob_ref, oc_ref, om_ref, oah_ref, obh_ref, och_ref, omh_ref,
                wout_ref, g_ref, wup_ref, cw_ref, cb_ref, wdown_ref, o_ref, mix_ref, h_ref, act_ref):
    tm = x_ref.shape[0]
    groups = ((oa_ref, oah_ref), (ob_ref, obh_ref), (oc_ref, och_ref), (om_ref, omh_ref))
    for g, (main_ref, halo_ref) in enumerate(groups):
        cs = slice(g * GROUP_WIDTH, (g + 1) * GROUP_WIDTH)
        mix_ref[0:HALO, cs] = halo_ref[...]
        mix_ref[HALO:, cs] = main_ref[...]
    upd = jnp.dot(mix_ref[...], wout_ref[...], preferred_element_type=F32)
    x = x_ref[...] + upd[HALO:]
    xh = xh_ref[...] + upd[0:HALO]
    keep = jnp.where(pl.program_id(0) % pos_blocks == 0, 0.0, 1.0)
    h_ref[0:HALO, :] = (_row_rms(xh, g_ref[...]) * keep).astype(BF16)
    h_ref[HALO:, :] = _row_rms(x, g_ref[...]).astype(BF16)
    h = h_ref[...]

    def conv(col0):
        cs = slice(col0, col0 + FFN_CHUNK)
        u = jnp.dot(h, wup_ref[:, cs], preferred_element_type=F32)
        y = cb_ref[:, cs] + cw_ref[2:3, cs] * u[HALO:HALO + tm]
        y = y + cw_ref[1:2, cs] * u[HALO - 1:HALO - 1 + tm]
        return y + cw_ref[0:1, cs] * u[HALO - 2:HALO - 2 + tm]

    for c in range(D_FF // FFN_CHUNK):
        gate = conv(c * FFN_CHUNK)
        val = conv(D_FF + c * FFN_CHUNK)
        act = gate * (1.0 / (1.0 + jnp.exp(-gate))) * val
        act_ref[:, c * FFN_CHUNK:(c + 1) * FFN_CHUNK] = act.astype(BF16)
    o_ref[...] = x + jnp.dot(act_ref[...], wdown_ref[...], preferred_element_type=F32)


def _ffn(x2, mixes, wout, g, wup, cw, cb, wdown, layer, seq):
    rows = x2.shape[0]
    tm = ROW_TILE
    pos_blocks = seq // tm
    hb = tm // HALO
    const = lambda shape: pl.BlockSpec(shape, lambda i: (0, 0), pipeline_mode=pl.Buffered(1))
    weight = lambda shape: pl.BlockSpec((None,) + shape, lambda i: (layer, 0, 0), pipeline_mode=pl.Buffered(1))
    halo = lambda i: (jnp.maximum(i * hb - 1, 0), 0)
    xspec = pl.BlockSpec((tm, D_MODEL), lambda i: (i, 0))
    return pl.pallas_call(
        functools.partial(_ffn_kernel, pos_blocks),
        grid=(rows // tm,),
        in_specs=[xspec, pl.BlockSpec((HALO, D_MODEL), halo)]
                 + [pl.BlockSpec((tm, GROUP_WIDTH), lambda i: (i, 0))] * len(mixes)
                 + [pl.BlockSpec((HALO, GROUP_WIDTH), halo)] * len(mixes)
                 + [weight((D_MODEL, D_MODEL)), const((1, D_MODEL)), weight((D_MODEL, 2 * D_FF)),
                    const((8, 2 * D_FF)), const((1, 2 * D_FF)), weight((D_FF, D_MODEL))],
        out_specs=xspec,
        out_shape=jax.ShapeDtypeStruct((rows, D_MODEL), F32),
        scratch_shapes=[pltpu.VMEM((HALO + tm, D_MODEL), BF16), pltpu.VMEM((HALO + tm, D_MODEL), BF16),
                        pltpu.VMEM((tm, D_FF), BF16)],
        compiler_params=_params(("parallel",)),
        name="ffn",
    )(x2, x2, *mixes, *mixes, wout, g, wup, cw, cb, wdown)


def _rope_tables(seq):
    inv = ROPE_THETA ** (-jnp.arange(0, HEAD_DIM, 2, dtype=F32) / HEAD_DIM)
    ang = jnp.arange(seq, dtype=F32)[:, None] * inv[None, :]
    ang = jnp.concatenate([ang, ang], axis=-1)
    sign = jnp.where(jnp.arange(HEAD_DIM) < HEAD_DIM // 2, -1.0, 1.0).astype(F32)
    cos = jnp.tile(jnp.cos(ang), (1, GROUP_HEADS))
    sin = jnp.tile(jnp.sin(ang) * sign[None, :], (1, GROUP_HEADS))
    return cos, sin


def _pad_rows(a, rows):
    return jnp.concatenate([a, jnp.zeros((rows - a.shape[0],) + a.shape[1:], a.dtype)], axis=0)


def kernel(x, mem, norm_mix, w_in, q_norm_a, k_norm_a, q_norm_b, k_norm_b, q_norm_m, k_norm_m,
           mem_norm, w_mem_kv, conv_w, conv_b, conv_ln_g, conv_ln_b, w_conv_out, out_norm, w_out,
           norm_ffn, w_up, ffn_conv_w, ffn_conv_b, w_down):
    b, seq, d = x.shape
    depth = w_in.shape[0]
    gw = GROUP_WIDTH
    cos, sin = _rope_tables(seq)
    x2 = x.reshape(b * seq, d)
    tile_h = lambda g: jnp.tile(g, GROUP_HEADS)
    r3 = lambda t: t.reshape(b, seq, gw)
    f2 = lambda t: t.reshape(b * seq, gw)
    w_in_b, w_mem_b, w_co_b = w_in.astype(BF16), w_mem_kv.astype(BF16), w_conv_out.astype(BF16)
    w_out_b, w_up_b, w_down_b = w_out.astype(BF16), w_up.astype(BF16), w_down.astype(BF16)
    for i in range(depth):
        gains = _pad_rows(jnp.stack([tile_h(q_norm_a[i]), tile_h(k_norm_a[i]), tile_h(q_norm_b[i]),
                                     tile_h(k_norm_b[i]), tile_h(q_norm_m[i])]), 8)
        wvat = w_in_b[i, :, 2 * gw:3 * gw].T
        gout = out_norm[i].reshape(4, 1, gw)
        qa, ka, vat, qbc, kb, kbc, vb, vbc, u, qm = _in_proj(x2, norm_mix[i][None], w_in_b, i, wvat, cos, sin,
                                                             gains, seq)
        oa = _moba(r3(qa), r3(ka), vat, gout[0])
        ob = _dilated(qbc, r3(kb), kbc, r3(vb), vbc, gout[1])
        oc = _conv_module(r3(u), _pad_rows(conv_w[i], CONV_PAD), conv_b[i][None], conv_ln_g[i][None],
                          conv_ln_b[i][None], w_co_b, i, gout[2])
        om = _mem_attn(r3(qm), mem, mem_norm[i][None], w_mem_b, i, tile_h(k_norm_m[i])[None], gout[3])
        x2 = _ffn(x2, [f2(oa), f2(ob), f2(oc), f2(om)], w_out_b, norm_ffn[i][None], w_up_b,
                  _pad_rows(ffn_conv_w[i], 8), ffn_conv_b[i][None], w_down_b, i, seq)
    return x2.reshape(b, seq, d)
```

```python
import functools

import jax
import jax.numpy as jnp
from jax import lax
from jax.experimental import pallas as pl
from jax.experimental.pallas import tpu as pltpu

F32 = jnp.float32
BF16 = jnp.bfloat16

D_MODEL = 1024
HEAD_DIM = 64
GROUP_WIDTH = 256
GROUP_HEADS = 4
N_PROJ_GROUPS = 9
MOBA_BLOCK = 256
MOBA_TOPK = 3
BAND = 128
DIL_MID = 4
DIL_MAX = 16
CONV_WIDTH = 31
FFN_CONV_WIDTH = 3
D_FF = 2816
ROPE_THETA = 10000.0
EPS = 1e-6
ATTN_SCALE = HEAD_DIM ** -0.5
Q_SCALE = ATTN_SCALE * 1.4426950408889634
NEG = -1e30

ROW_TILE = 1024
FFN_CHUNK = 256
HALO = 16
CONV_PAD = 32
CONV_ROWS = 512
PIPE_DEPTH = 3
PROJ_DEPTH = 1
STAT_ROWS = 8
VMEM_LIMIT = 56 * 1024 * 1024


def _params(sem):
    return pltpu.CompilerParams(dimension_semantics=sem, vmem_limit_bytes=VMEM_LIMIT)


def _lane_head(shape):
    return lax.broadcasted_iota(jnp.int32, shape, 1) // HEAD_DIM


def _head_ones():
    r = lax.broadcasted_iota(jnp.int32, (GROUP_WIDTH, GROUP_WIDTH), 0) // HEAD_DIM
    c = lax.broadcasted_iota(jnp.int32, (GROUP_WIDTH, GROUP_WIDTH), 1) // HEAD_DIM
    return jnp.where(r == c, 1.0, 0.0).astype(BF16)


def _head_rms(p, gain, ones_bd):
    ss = jnp.dot((p * p).astype(BF16), ones_bd, preferred_element_type=F32)
    return p * lax.rsqrt(ss * (1.0 / HEAD_DIM) + EPS) * gain


def _row_rms(x, gain):
    return x * lax.rsqrt(jnp.mean(x * x, axis=-1, keepdims=True) + EPS) * gain


def _dot_nt(a, b):
    return lax.dot_general(a, b, (((1,), (1,)), ((), ())), preferred_element_type=F32)


def _stack_heads(k):
    lh = _lane_head((1, GROUP_WIDTH))
    return jnp.concatenate([k * jnp.where(lh == h, 1.0, 0.0).astype(BF16) for h in range(GROUP_HEADS)], axis=0)


def _head_slice(h):
    return slice(h * HEAD_DIM, (h + 1) * HEAD_DIM)


def _transpose_bf16(v):
    return v.astype(F32).T.astype(BF16)


class _Pipe:
    def __init__(self, depth=3):
        self.depth = depth
        self.items = []

    def push(self, score, finish):
        self.items.append((score(), finish))
        if len(self.items) > self.depth:
            s, f = self.items.pop(0)
            f(s)

    def flush(self):
        for s, f in self.items:
            f(s)
        self.items = []


def _scores(k_st, q):
    return _dot_nt(k_st, q).astype(BF16)


def _col_max(s):
    tk, tq = s.shape
    part = jnp.max(s.reshape(tk // 16, 16, tq), axis=0)
    return jnp.max(part.astype(F32), axis=0, keepdims=True)


def _exp_pv(s, shift, vt_h):
    p = jnp.exp2(s - shift.astype(BF16))
    ones = jnp.ones((2 * STAT_ROWS, vt_h.shape[1]), BF16)
    pv = jnp.dot(jnp.concatenate([vt_h, ones], axis=0), p, preferred_element_type=F32)
    return pv[:HEAD_DIM], pv[HEAD_DIM:HEAD_DIM + 1]


def _softmax_pv(s, vt_h):
    m = _col_max(s)
    o, l = _exp_pv(s, m, vt_h)
    return o, m, l


def _head_rows(parts):
    tq = parts[0][0].shape[1]
    rows = lambda xs: jnp.concatenate(xs, axis=0).T
    return (rows([o for o, _, _ in parts]),
            rows([jnp.broadcast_to(m, (HEAD_DIM, tq)) for _, m, _ in parts]),
            rows([jnp.broadcast_to(l, (HEAD_DIM, tq)) for _, _, l in parts]))


def _merge(acc0, m0, l0, acc1, m1, l1):
    mn = jnp.maximum(m0, m1)
    a0 = jnp.exp2(m0 - mn)
    a1 = jnp.exp2(m1 - mn)
    return acc0 * a0 + acc1 * a1, mn, l0 * a0 + l1 * a1


def _in_proj_kernel(x_ref, g_ref, w_ref, wvat_ref, cos_ref, sin_ref, gains_ref,
                    qa_ref, ka_ref, vat_ref, qbc_ref, kb_ref, kbc_ref, vb_ref, vbc_ref, u_ref, qm_ref,
                    perm_ref):
    h = _row_rms(x_ref[...], g_ref[...]).astype(BF16)
    ones_bd = _head_ones()
    cos = cos_ref[...]
    sin = sin_ref[...]
    tm = h.shape[0]
    lower = (lax.broadcasted_iota(jnp.int32, (tm, GROUP_WIDTH), 1) % HEAD_DIM) < (HEAD_DIM // 2)

    def proj(j):
        return jnp.dot(h, w_ref[:, j * GROUP_WIDTH:(j + 1) * GROUP_WIDTH], preferred_element_type=F32)

    def rope(y):
        rot = jnp.where(lower, pltpu.roll(y, GROUP_WIDTH - HEAD_DIM // 2, 1), pltpu.roll(y, HEAD_DIM // 2, 1))
        return y * cos + rot * sin

    def class_major(y, out_ref):
        for half in range(GROUP_WIDTH // BAND):
            ls = slice(half * BAND, (half + 1) * BAND)
            perm_ref[half] = y[:, ls]
            for r in range(DIL_MAX):
                out_ref[0, r, :, ls] = perm_ref[half, pl.ds(r, tm // DIL_MAX, stride=DIL_MAX), :].astype(BF16)

    def head_norm(y, gain_row):
        return _head_rms(y, gains_ref[gain_row:gain_row + 1, :], ones_bd)

    def fin_qa(y):
        qa_ref[...] = (rope(head_norm(y, 0)) * Q_SCALE).astype(BF16)

    def fin_ka(y):
        ka_ref[...] = rope(head_norm(y, 1)).astype(BF16)

    def fin_vat(y):
        vat = y.astype(BF16)
        for j in range(tm // MOBA_BLOCK):
            vat_ref[0, j] = vat[:, j * MOBA_BLOCK:(j + 1) * MOBA_BLOCK]

    def fin_qb(y):
        class_major(rope(head_norm(y, 2)) * Q_SCALE, qbc_ref)

    def fin_kb(y):
        kb = rope(head_norm(y, 3))
        kb_ref[...] = kb.astype(BF16)
        class_major(kb, kbc_ref)

    def fin_vb(y):
        vb_ref[...] = y.astype(BF16)
        class_major(y, vbc_ref)

    def fin_glu(ys):
        c_val, c_gate = ys
        u_ref[...] = (c_val * (1.0 / (1.0 + jnp.exp(-c_gate)))).astype(BF16)

    def fin_qm(y):
        qm_ref[...] = (head_norm(y, 4) * Q_SCALE).astype(BF16)

    pipe = _Pipe(PROJ_DEPTH)
    pipe.push(lambda: proj(0), fin_qa)
    pipe.push(lambda: proj(1), fin_ka)
    pipe.push(lambda: _dot_nt(wvat_ref[...], h), fin_vat)
    pipe.push(lambda: proj(3), fin_qb)
    pipe.push(lambda: proj(4), fin_kb)
    pipe.push(lambda: proj(5), fin_vb)
    pipe.push(lambda: (proj(6), proj(7)), fin_glu)
    pipe.push(lambda: proj(8), fin_qm)
    pipe.flush()


def _layer_spec(layer, shape):
    return pl.BlockSpec((None,) + shape, lambda *_: (layer, 0, 0))


def _in_proj(x2, g, w, layer, wvat, cos, sin, gains, seq):
    rows = x2.shape[0]
    b = rows // seq
    tm = ROW_TILE
    pos_blocks = seq // tm
    nat = jax.ShapeDtypeStruct((rows, GROUP_WIDTH), BF16)
    nat_spec = pl.BlockSpec((tm, GROUP_WIDTH), lambda i: (i, 0))
    cm = jax.ShapeDtypeStruct((b, DIL_MAX, seq // DIL_MAX, GROUP_WIDTH), BF16)
    cm_spec = pl.BlockSpec((1, DIL_MAX, tm // DIL_MAX, GROUP_WIDTH),
                           lambda i: (i // pos_blocks, 0, i % pos_blocks, 0))
    vat = jax.ShapeDtypeStruct((b, seq // MOBA_BLOCK, GROUP_WIDTH, MOBA_BLOCK), BF16)
    vat_spec = pl.BlockSpec((1, tm // MOBA_BLOCK, GROUP_WIDTH, MOBA_BLOCK),
                            lambda i: (i // pos_blocks, i % pos_blocks, 0, 0))
    return pl.pallas_call(
        _in_proj_kernel,
        grid=(rows // tm,),
        in_specs=[
            pl.BlockSpec((tm, D_MODEL), lambda i: (i, 0)),
            pl.BlockSpec((1, D_MODEL), lambda i: (0, 0)),
            _layer_spec(layer, (D_MODEL, N_PROJ_GROUPS * GROUP_WIDTH)),
            pl.BlockSpec((GROUP_WIDTH, D_MODEL), lambda i: (0, 0)),
            pl.BlockSpec((tm, GROUP_WIDTH), lambda i: (i % pos_blocks, 0)),
            pl.BlockSpec((tm, GROUP_WIDTH), lambda i: (i % pos_blocks, 0)),
            pl.BlockSpec((8, GROUP_WIDTH), lambda i: (0, 0)),
        ],
        out_specs=[nat_spec, nat_spec, vat_spec, cm_spec, nat_spec, cm_spec, nat_spec, cm_spec, nat_spec, nat_spec],
        out_shape=[nat, nat, vat, cm, nat, cm, nat, cm, nat, nat],
        scratch_shapes=[pltpu.VMEM((GROUP_WIDTH // BAND, tm, BAND), F32)],
        compiler_params=_params(("parallel",)),
        name="in_proj",
    )(x2, g, w, wvat, cos, sin, gains)


def _moba_kernel(q_ref, k_ref, vt_ref, gout_ref, qm_ref, mem_ref, gmem_ref, wkv_ref, gk_ref, goutm_ref,
                 o_ref, om_ref):
    seq = k_ref.shape[1]
    nb = seq // MOBA_BLOCK
    tq = MOBA_BLOCK
    nrow = 2 * STAT_ROWS

    r = lax.broadcasted_iota(jnp.int32, (nrow, seq), 0)
    c = lax.broadcasted_iota(jnp.int32, (nrow, seq), 1) // MOBA_BLOCK
    avg = jnp.where(r == c, 1.0 / MOBA_BLOCK, 0.0).astype(BF16)
    kmean = jnp.dot(avg, k_ref[0], preferred_element_type=F32)
    khi = kmean.astype(BF16)
    klo = (kmean - khi.astype(F32)).astype(BF16)
    khi_st = _stack_heads(khi)
    klo_st = _stack_heads(klo)

    kk = lax.broadcasted_iota(jnp.int32, (tq, tq), 0)
    qq = lax.broadcasted_iota(jnp.int32, (tq, tq), 1)
    causal = jnp.where(kk <= qq, 0.0, NEG).astype(BF16)
    blk = lax.broadcasted_iota(jnp.int32, (nrow, tq), 0)
    gout = gout_ref[...]
    pipe = _Pipe(PIPE_DEPTH)
    kst = {}

    mh = _row_rms(mem_ref[0], gmem_ref[...]).astype(BF16)
    kv = jnp.dot(mh, wkv_ref[...], preferred_element_type=F32)
    km_st = _stack_heads(_head_rms(kv[:, :GROUP_WIDTH], gk_ref[...], _head_ones()).astype(BF16))
    vmt = kv[:, GROUP_WIDTH:].T.astype(BF16)
    n_mem = mem_ref.shape[1]
    goutm = goutm_ref[...]

    def mem_tile(qs):
        def finish(s_all):
            parts = [_softmax_pv(s_all[h * n_mem:(h + 1) * n_mem], vmt[_head_slice(h), :])
                     for h in range(GROUP_HEADS)]
            ot = jnp.concatenate([o / l for o, _, l in parts], axis=0)
            om_ref[0, qs, :] = _row_rms(ot.T, goutm).astype(BF16)

        pipe.push(lambda: _scores(km_st, qm_ref[0, qs, :]), finish)

    def stacked(n):
        if n not in kst:
            kst[n] = _stack_heads(k_ref[0, n * tq:(n + 1) * tq, :])
        return kst[n]

    for i in range(nb):
        qs = slice(i * tq, (i + 1) * tq)
        q = q_ref[0, qs, :]
        state = [None] * GROUP_HEADS

        selb = [None] * GROUP_HEADS
        if i > MOBA_TOPK:
            past = blk < i
            g_all = _dot_nt(khi_st, q) + _dot_nt(klo_st, q)
            for h in range(GROUP_HEADS):
                g = jnp.where(past, g_all[h * nrow:(h + 1) * nrow], -jnp.inf)
                rank = jnp.zeros((nrow, tq), F32)
                for mth in range(i):
                    row = g[mth:mth + 1, :]
                    rank = rank + jnp.where(row == g, jnp.where(blk > mth, 1.0, 0.0),
                                            jnp.where(row > g, 1.0, 0.0))
                selb[h] = jnp.where(rank < MOBA_TOPK, 0.0, NEG)

        def finish_own(s_all, state=state, i=i):
            for h in range(GROUP_HEADS):
                s = s_all[h * tq:(h + 1) * tq] + causal
                state[h] = _softmax_pv(s, vt_ref[0, i, _head_slice(h), :])

        def finish_past(s_all, n, state=state, selb=selb):
            for h in range(GROUP_HEADS):
                s = s_all[h * tq:(h + 1) * tq]
                acc, m_old, l_old = state[h]
                smax = _col_max(s)
                if selb[h] is None:
                    m_new = jnp.maximum(m_old, smax)
                    shift = m_new
                else:
                    bias = selb[h][n:n + 1, :]
                    m_new = jnp.maximum(m_old, smax + bias)
                    shift = m_new - bias
                alpha = jnp.exp2(m_old - m_new)
                pv, l_new = _exp_pv(s, shift, vt_ref[0, n, _head_slice(h), :])
                state[h] = (alpha * acc + pv, m_new, alpha * l_old + l_new)

        def finalize(state=state, qs=qs):
            ot = jnp.concatenate([acc / l for acc, _, l in state], axis=0)
            o_ref[0, qs, :] = _row_rms(ot.T, gout).astype(BF16)

        order = [i] + list(range(i))
        for pos, n in enumerate(order):
            score = lambda n=n, q=q: _scores(stacked(n), q)
            if n == i:
                fin = finish_own
            else:
                fin = lambda s, n=n, f=finish_past: f(s, n)
            if pos == len(order) - 1:
                fin = lambda s, fin=fin, fz=finalize: (fin(s), fz())
            pipe.push(score, fin)
        mem_tile(qs)
    pipe.flush()


def _moba(q, k, vt, gout, qm, mem, gmem, wkv, layer, gk, goutm):
    b, seq, gw = q.shape
    nb = seq // MOBA_BLOCK
    n_mem = mem.shape[1]
    nat = pl.BlockSpec((1, seq, gw), lambda bi: (bi, 0, 0))
    const = lambda shape: pl.BlockSpec(shape, lambda bi: (0, 0))
    return pl.pallas_call(
        _moba_kernel,
        grid=(b,),
        in_specs=[nat, nat, pl.BlockSpec((1, nb, gw, MOBA_BLOCK), lambda bi: (bi, 0, 0, 0)), const((1, gw)),
                  nat, pl.BlockSpec((1, n_mem, D_MODEL), lambda bi: (bi, 0, 0)), const((1, D_MODEL)),
                  _layer_spec(layer, (D_MODEL, 2 * gw)), const((1, gw)), const((1, gw))],
        out_specs=[nat, nat],
        out_shape=[jax.ShapeDtypeStruct((b, seq, gw), BF16)] * 2,
        compiler_params=_params(("parallel",)),
        name="moba",
    )(q, k, vt, gout, qm, mem, gmem, wkv, gk, goutm)


def _dilated_kernel(qc_ref, k1_ref, kc_ref, v1_ref, vc_ref, b1_ref, b4_ref, b16_ref, gout_ref, o_ref,
                    acc_ref, m_ref, l_ref, v1t_ref, fin_ref):
    seq = k1_ref.shape[1]
    per = DIL_MAX // DIL_MID
    gout = gout_ref[...]
    pipe = _Pipe(PIPE_DEPTH)

    def tile(q, k, vt, bias, done):
        tk = k.shape[0]

        def finish(s_all):
            parts = [_softmax_pv(s_all[h * tk:(h + 1) * tk] + bias, vt[_head_slice(h), :])
                     for h in range(GROUP_HEADS)]
            done(*_head_rows(parts))

        pipe.push(lambda: _scores(_stack_heads(k), q), finish)

    for c0 in range(0, seq, 2 * BAND):
        v1t_ref[:, c0:c0 + 2 * BAND] = _transpose_bf16(v1_ref[0, c0:c0 + 2 * BAND, :])

    for r in range(DIL_MAX):
        def done16(acc, m_all, l_all, r=r):
            acc_ref[r] = acc
            m_ref[r] = m_all
            l_ref[r] = l_all

        tile(qc_ref[0, r], kc_ref[0, r], _transpose_bf16(vc_ref[0, r]), b16_ref[...], done16)

    jq4 = BAND // per
    for a in range(DIL_MID):
        classes = [u * DIL_MID + a for u in range(per)]
        for jb in range(seq // DIL_MID // BAND):
            rs = slice(jb * jq4, (jb + 1) * jq4)
            if jb == 0:
                ks = rs
                bias = jnp.concatenate([b4_ref[2 * jq4 * u + jq4:2 * jq4 * (u + 1), :] for u in range(per)], axis=0)
            else:
                ks = slice((jb - 1) * jq4, (jb + 1) * jq4)
                bias = b4_ref[...]

            def done4(a1, m1, l1, classes=classes, rs=rs):
                for u, c in enumerate(classes):
                    ts = slice(u * jq4, (u + 1) * jq4)
                    acc, m_all, l_all = _merge(acc_ref[c, rs, :], m_ref[c, rs, :], l_ref[c, rs, :],
                                               a1[ts], m1[ts], l1[ts])
                    acc_ref[c, rs, :] = acc
                    m_ref[c, rs, :] = m_all
                    l_ref[c, rs, :] = l_all

            v = jnp.concatenate([vc_ref[0, c, ks, :] for c in classes], axis=0)
            tile(jnp.concatenate([qc_ref[0, c, rs, :] for c in classes], axis=0),
                 jnp.concatenate([kc_ref[0, c, ks, :] for c in classes], axis=0),
                 _transpose_bf16(v), bias, done4)

    tq = 2 * BAND
    jq1 = tq // DIL_MAX
    for jb in range(seq // tq):
        rs = slice(jb * jq1, (jb + 1) * jq1)
        if jb == 0:
            ks = slice(0, tq)
            bias = b1_ref[BAND:, :]
        else:
            ks = slice(jb * tq - BAND, (jb + 1) * tq)
            bias = b1_ref[...]

        def done1(a1, m1, l1, rs=rs, jb=jb):
            for r in range(DIL_MAX):
                ts = slice(r * jq1, (r + 1) * jq1)
                acc, _, l_all = _merge(acc_ref[r, rs, :], m_ref[r, rs, :], l_ref[r, rs, :],
                                       a1[ts], m1[ts], l1[ts])
                res = _row_rms(acc / l_all, gout)
                for half in range(GROUP_WIDTH // BAND):
                    fin_ref[half, pl.ds(r, jq1, stride=DIL_MAX), :] = res[:, half * BAND:(half + 1) * BAND]
            for half in range(GROUP_WIDTH // BAND):
                o_ref[0, jb * tq:(jb + 1) * tq, half * BAND:(half + 1) * BAND] = fin_ref[half].astype(BF16)

        tile(jnp.concatenate([qc_ref[0, r, rs, :] for r in range(DIL_MAX)], axis=0),
             k1_ref[0, ks, :], v1t_ref[:, ks], bias, done1)
    pipe.flush()


def _dilated_biases():
    def band(krel, qrel):
        dist = qrel[None, :] - krel[:, None]
        return jnp.where((dist >= 0) & (dist <= BAND), 0.0, NEG).astype(BF16)

    tq = 2 * BAND
    jq = tq // DIL_MAX
    col = jnp.arange(tq)
    b1 = band(jnp.arange(BAND + tq) - BAND, DIL_MAX * (col % jq) + col // jq)
    per = DIL_MAX // DIL_MID
    jq = BAND // per
    col = jnp.arange(BAND)
    row = jnp.arange(2 * BAND)
    b4 = band(per * (row % (2 * jq) - jq) + row // (2 * jq), per * (col % jq) + col // jq)
    b16 = band(jnp.arange(BAND), jnp.arange(BAND))
    return b1, b4, b16


def _dilated(qc, k1, kc, v1, vc, gout):
    b, seq, gw = k1.shape
    b1, b4, b16 = _dilated_biases()
    nat = pl.BlockSpec((1, seq, gw), lambda bi: (bi, 0, 0))
    cm_shape = (DIL_MAX, seq // DIL_MAX, gw)
    cm = pl.BlockSpec((1,) + cm_shape, lambda bi: (bi, 0, 0, 0))
    const = lambda shape: pl.BlockSpec(shape, lambda bi: (0, 0))
    return pl.pallas_call(
        _dilated_kernel,
        grid=(b,),
        in_specs=[cm, nat, cm, nat, cm, const(b1.shape), const(b4.shape), const(b16.shape), const((1, gw))],
        out_specs=nat,
        out_shape=jax.ShapeDtypeStruct((b, seq, gw), BF16),
        scratch_shapes=[pltpu.VMEM(cm_shape, F32)] * 3 + [pltpu.VMEM((gw, seq), BF16),
                                                          pltpu.VMEM((gw // BAND, 2 * BAND, BAND), F32)],
        compiler_params=_params(("parallel",)),
        name="dilated",
    )(qc, k1, kc, v1, vc, b1, b4, b16, gout)


def _conv_kernel(u_ref, w_ref, b_ref, lng_ref, lnb_ref, wout_ref, gout_ref, o_ref, pad_ref):
    seq = u_ref.shape[1]
    pad_ref[0:CONV_PAD, :] = jnp.zeros((CONV_PAD, GROUP_WIDTH), F32)
    pad_ref[CONV_PAD:CONV_PAD + seq, :] = u_ref[0].astype(F32)
    pad_ref[CONV_PAD + seq:, :] = jnp.zeros((16, GROUP_WIDTH), F32)
    lead = CONV_PAD - (CONV_WIDTH - 1)
    span = CONV_ROWS + 16
    for c0 in range(0, seq, CONV_ROWS):
        y = jnp.broadcast_to(b_ref[...], (CONV_ROWS, GROUP_WIDTH))
        for sub in range(8):
            z = None
            for a in range((CONV_WIDTH - sub + 7) // 8):
                j = 8 * a + sub
                t = w_ref[j:j + 1, :] * pad_ref[c0 + 8 * a:c0 + 8 * a + span, :]
                z = t if z is None else z + t
            y = y + z[lead + sub:lead + sub + CONV_ROWS]
        mu = jnp.mean(y, axis=-1, keepdims=True)
        d = y - mu
        var = jnp.mean(d * d, axis=-1, keepdims=True)
        yn = d * lax.rsqrt(var + EPS) * lng_ref[...] + lnb_ref[...]
        act = yn * (1.0 / (1.0 + jnp.exp(-yn)))
        oc = jnp.dot(act.astype(BF16), wout_ref[...], preferred_element_type=F32)
        o_ref[0, c0:c0 + CONV_ROWS, :] = _row_rms(oc, gout_ref[...]).astype(BF16)


def _conv_module(u, w, bias, lng, lnb, wout, layer, gout):
    b, seq, gw = u.shape
    const = lambda shape: pl.BlockSpec(shape, lambda bi: (0, 0))
    return pl.pallas_call(
        _conv_kernel,
        grid=(b,),
        in_specs=[pl.BlockSpec((1, seq, gw), lambda bi: (bi, 0, 0)),
                  const((CONV_PAD, gw)), const((1, gw)), const((1, gw)), const((1, gw)),
                  _layer_spec(layer, (gw, gw)), const((1, gw))],
        out_specs=pl.BlockSpec((1, seq, gw), lambda bi: (bi, 0, 0)),
        out_shape=jax.ShapeDtypeStruct((b, seq, gw), BF16),
        scratch_shapes=[pltpu.VMEM((CONV_PAD + seq + 16, gw), F32)],
        compiler_params=_params(("parallel",)),
        name="conv_module",
    )(u, w, bias, lng, lnb, wout, gout)


def _ffn_kernel(pos_blocks, x_ref, xh_ref, oa_ref, ob_ref, oc_ref, om_ref, oah_ref, obh_ref, och_ref, omh_ref,
                wout_ref, g_ref, wup_ref, cw_ref, cb_ref, wdown_ref, o_ref, mix_ref, h_ref, act_ref):
    tm = x_ref.shape[0]
    groups = ((oa_ref, oah_ref), (ob_ref, obh_ref), (oc_ref, och_ref), (om_ref, omh_ref))
    for g, (main_ref, halo_ref) in enumerate(groups):
        cs = slice(g * GROUP_WIDTH, (g + 1) * GROUP_WIDTH)
        mix_ref[0:HALO, cs] = halo_ref[...]
        mix_ref[HALO:, cs] = main_ref[...]
    upd = jnp.dot(mix_ref[...], wout_ref[...], preferred_element_type=F32)
    x = x_ref[...] + upd[HALO:]
    xh = xh_ref[...] + upd[0:HALO]
    keep = jnp.where(pl.program_id(0) % pos_blocks == 0, 0.0, 1.0)
    h_ref[0:HALO, :] = (_row_rms(xh, g_ref[...]) * keep).astype(BF16)
    h_ref[HALO:, :] = _row_rms(x, g_ref[...]).astype(BF16)
    h = h_ref[...]

    def conv(col0):
        cs = slice(col0, col0 + FFN_CHUNK)
        u = jnp.dot(h, wup_ref[:, cs], preferred_element_type=F32)
        y = cb_ref[:, cs] + cw_ref[2:3, cs] * u[HALO:HALO + tm]
        y = y + cw_ref[1:2, cs] * u[HALO - 1:HALO - 1 + tm]
        return y + cw_ref[0:1, cs] * u[HALO - 2:HALO - 2 + tm]

    for c in range(D_FF // FFN_CHUNK):
        gate = conv(c * FFN_CHUNK)
        val = conv(D_FF + c * FFN_CHUNK)
        act = gate * (1.0 / (1.0 + jnp.exp(-gate))) * val
        act_ref[:, c * FFN_CHUNK:(c + 1) * FFN_CHUNK] = act.astype(BF16)
    o_ref[...] = x + jnp.dot(act_ref[...], wdown_ref[...], preferred_element_type=F32)


def _ffn(x2, mixes, wout, g, wup, cw, cb, wdown, layer, seq):
    rows = x2.shape[0]
    tm = ROW_TILE
    pos_blocks = seq // tm
    hb = tm // HALO
    const = lambda shape: pl.BlockSpec(shape, lambda i: (0, 0), pipeline_mode=pl.Buffered(1))
    weight = lambda shape: pl.BlockSpec((None,) + shape, lambda i: (layer, 0, 0), pipeline_mode=pl.Buffered(1))
    halo = lambda i: (jnp.maximum(i * hb - 1, 0), 0)
    xspec = pl.BlockSpec((tm, D_MODEL), lambda i: (i, 0))
    return pl.pallas_call(
        functools.partial(_ffn_kernel, pos_blocks),
        grid=(rows // tm,),
        in_specs=[xspec, pl.BlockSpec((HALO, D_MODEL), halo)]
                 + [pl.BlockSpec((tm, GROUP_WIDTH), lambda i: (i, 0))] * len(mixes)
                 + [pl.BlockSpec((HALO, GROUP_WIDTH), halo)] * len(mixes)
                 + [weight((D_MODEL, D_MODEL)), const((1, D_MODEL)), weight((D_MODEL, 2 * D_FF)),
                    const((8, 2 * D_FF)), const((1, 2 * D_FF)), weight((D_FF, D_MODEL))],
        out_specs=xspec,
        out_shape=jax.ShapeDtypeStruct((rows, D_MODEL), F32),
        scratch_shapes=[pltpu.VMEM((HALO + tm, D_MODEL), BF16), pltpu.VMEM((HALO + tm, D_MODEL), BF16),
                        pltpu.VMEM((tm, D_FF), BF16)],
        compiler_params=_params(("parallel",)),
        name="ffn",
    )(x2, x2, *mixes, *mixes, wout, g, wup, cw, cb, wdown)


def _rope_tables(seq):
    inv = ROPE_THETA ** (-jnp.arange(0, HEAD_DIM, 2, dtype=F32) / HEAD_DIM)
    ang = jnp.arange(seq, dtype=F32)[:, None] * inv[None, :]
    ang = jnp.concatenate([ang, ang], axis=-1)
    sign = jnp.where(jnp.arange(HEAD_DIM) < HEAD_DIM // 2, -1.0, 1.0).astype(F32)
    cos = jnp.tile(jnp.cos(ang), (1, GROUP_HEADS))
    sin = jnp.tile(jnp.sin(ang) * sign[None, :], (1, GROUP_HEADS))
    return cos, sin


def _pad_rows(a, rows):
    return jnp.concatenate([a, jnp.zeros((rows - a.shape[0],) + a.shape[1:], a.dtype)], axis=0)


def kernel(x, mem, norm_mix, w_in, q_norm_a, k_norm_a, q_norm_b, k_norm_b, q_norm_m, k_norm_m,
           mem_norm, w_mem_kv, conv_w, conv_b, conv_ln_g, conv_ln_b, w_conv_out, out_norm, w_out,
           norm_ffn, w_up, ffn_conv_w, ffn_conv_b, w_down):
    b, seq, d = x.shape
    depth = w_in.shape[0]
    gw = GROUP_WIDTH
    cos, sin = _rope_tables(seq)
    x2 = x.reshape(b * seq, d)
    tile_h = lambda g: jnp.tile(g, GROUP_HEADS)
    r3 = lambda t: t.reshape(b, seq, gw)
    f2 = lambda t: t.reshape(b * seq, gw)
    w_in_b, w_mem_b, w_co_b = w_in.astype(BF16), w_mem_kv.astype(BF16), w_conv_out.astype(BF16)
    w_out_b, w_up_b, w_down_b = w_out.astype(BF16), w_up.astype(BF16), w_down.astype(BF16)
    for i in range(depth):
        gains = _pad_rows(jnp.stack([tile_h(q_norm_a[i]), tile_h(k_norm_a[i]), tile_h(q_norm_b[i]),
                                     tile_h(k_norm_b[i]), tile_h(q_norm_m[i])]), 8)
        wvat = w_in[i, :, 2 * gw:3 * gw].T.astype(BF16)
        gout = out_norm[i].reshape(4, 1, gw)
        qa, ka, vat, qbc, kb, kbc, vb, vbc, u, qm = _in_proj(x2, norm_mix[i][None], w_in_b, i, wvat, cos, sin,
                                                             gains, seq)
        oa, om = _moba(r3(qa), r3(ka), vat, gout[0], r3(qm), mem, mem_norm[i][None], w_mem_b, i,
                       tile_h(k_norm_m[i])[None], gout[3])
        ob = _dilated(qbc, r3(kb), kbc, r3(vb), vbc, gout[1])
        oc = _conv_module(r3(u), _pad_rows(conv_w[i], CONV_PAD), conv_b[i][None], conv_ln_g[i][None],
                          conv_ln_b[i][None], w_co_b, i, gout[2])
        x2 = _ffn(x2, [f2(oa), f2(ob), f2(oc), f2(om)], w_out_b, norm_ffn[i][None], w_up_b,
                  _pad_rows(ffn_conv_w[i], 8), ffn_conv_b[i][None], w_down_b, i, seq)
    return x2.reshape(b, seq, d)
```

```python
import functools

import jax
import jax.numpy as jnp
from jax import lax
from jax.experimental import pallas as pl
from jax.experimental.pallas import tpu as pltpu

F32 = jnp.float32
BF16 = jnp.bfloat16

D_MODEL = 1024
HEAD_DIM = 64
GROUP_WIDTH = 256
GROUP_HEADS = 4
N_PROJ_GROUPS = 9
MOBA_BLOCK = 256
MOBA_TOPK = 3
BAND = 128
DIL_MID = 4
DIL_MAX = 16
CONV_WIDTH = 31
FFN_CONV_WIDTH = 3
D_FF = 2816
ROPE_THETA = 10000.0
EPS = 1e-6
ATTN_SCALE = HEAD_DIM ** -0.5
Q_SCALE = ATTN_SCALE * 1.4426950408889634
NEG = -1e30

ROW_TILE = 1024
FFN_CHUNK = 256
HALO = 16
CONV_PAD = 32
CONV_ROWS = 1024
PIPE_DEPTH = 3
PROJ_DEPTH = 1
STAT_ROWS = 8
VMEM_LIMIT = 56 * 1024 * 1024


def _params(sem):
    return pltpu.CompilerParams(dimension_semantics=sem, vmem_limit_bytes=VMEM_LIMIT)


def _lane_head(shape):
    return lax.broadcasted_iota(jnp.int32, shape, 1) // HEAD_DIM


def _head_ones():
    r = lax.broadcasted_iota(jnp.int32, (GROUP_WIDTH, GROUP_WIDTH), 0) // HEAD_DIM
    c = lax.broadcasted_iota(jnp.int32, (GROUP_WIDTH, GROUP_WIDTH), 1) // HEAD_DIM
    return jnp.where(r == c, 1.0, 0.0).astype(BF16)


def _head_rms(p, gain, ones_bd):
    ss = jnp.dot((p * p).astype(BF16), ones_bd, preferred_element_type=F32)
    return p * lax.rsqrt(ss * (1.0 / HEAD_DIM) + EPS) * gain


def _row_rms(x, gain):
    return x * lax.rsqrt(jnp.mean(x * x, axis=-1, keepdims=True) + EPS) * gain


def _dot_nt(a, b):
    return lax.dot_general(a, b, (((1,), (1,)), ((), ())), preferred_element_type=F32)


def _stack_heads(k):
    lh = _lane_head((1, GROUP_WIDTH))
    return jnp.concatenate([k * jnp.where(lh == h, 1.0, 0.0).astype(BF16) for h in range(GROUP_HEADS)], axis=0)


def _head_slice(h):
    return slice(h * HEAD_DIM, (h + 1) * HEAD_DIM)


def _transpose_bf16(v):
    return v.astype(F32).T.astype(BF16)


class _Pipe:
    def __init__(self, depth=3):
        self.depth = depth
        self.items = []

    def push(self, score, finish):
        self.items.append((score(), finish))
        if len(self.items) > self.depth:
            s, f = self.items.pop(0)
            f(s)

    def flush(self):
        for s, f in self.items:
            f(s)
        self.items = []


def _scores(k_st, q):
    return _dot_nt(k_st, q).astype(BF16)


def _col_max(s):
    tk, tq = s.shape
    part = jnp.max(s.reshape(tk // 16, 16, tq), axis=0)
    return jnp.max(part.astype(F32), axis=0, keepdims=True)


def _exp_pv(s, shift, vt_h):
    p = jnp.exp2(s - shift.astype(BF16))
    ones = jnp.ones((2 * STAT_ROWS, vt_h.shape[1]), BF16)
    pv = jnp.dot(jnp.concatenate([vt_h, ones], axis=0), p, preferred_element_type=F32)
    return pv[:HEAD_DIM], pv[HEAD_DIM:HEAD_DIM + 1]


def _softmax_pv(s, vt_h):
    m = _col_max(s)
    o, l = _exp_pv(s, m, vt_h)
    return o, m, l


def _head_rows(parts):
    tq = parts[0][0].shape[1]
    rows = lambda xs: jnp.concatenate(xs, axis=0).T
    return (rows([o for o, _, _ in parts]),
            rows([jnp.broadcast_to(m, (HEAD_DIM, tq)) for _, m, _ in parts]),
            rows([jnp.broadcast_to(l, (HEAD_DIM, tq)) for _, _, l in parts]))


def _merge(acc0, m0, l0, acc1, m1, l1):
    mn = jnp.maximum(m0, m1)
    a0 = jnp.exp2(m0 - mn)
    a1 = jnp.exp2(m1 - mn)
    return acc0 * a0 + acc1 * a1, mn, l0 * a0 + l1 * a1


def _in_proj_kernel(x_ref, g_ref, w_ref, cos_ref, sin_ref, gains_ref,
                    qa_ref, ka_ref, vat_ref, qbc_ref, kb_ref, kbc_ref, vb_ref, vbc_ref, u_ref, qm_ref,
                    perm_ref):
    h = _row_rms(x_ref[...], g_ref[...]).astype(BF16)
    ones_bd = _head_ones()
    cos = cos_ref[...]
    sin = sin_ref[...]
    tm = h.shape[0]
    lower = (lax.broadcasted_iota(jnp.int32, (tm, GROUP_WIDTH), 1) % HEAD_DIM) < (HEAD_DIM // 2)

    def proj(j):
        return jnp.dot(h, w_ref[:, j * GROUP_WIDTH:(j + 1) * GROUP_WIDTH], preferred_element_type=F32)

    def rope(y):
        rot = jnp.where(lower, pltpu.roll(y, GROUP_WIDTH - HEAD_DIM // 2, 1), pltpu.roll(y, HEAD_DIM // 2, 1))
        return y * cos + rot * sin

    def class_major(y, out_ref):
        for half in range(GROUP_WIDTH // BAND):
            ls = slice(half * BAND, (half + 1) * BAND)
            perm_ref[half] = y[:, ls]
            for r in range(DIL_MAX):
                out_ref[0, r, :, ls] = perm_ref[half, pl.ds(r, tm // DIL_MAX, stride=DIL_MAX), :].astype(BF16)

    def head_norm(y, gain_row):
        return _head_rms(y, gains_ref[gain_row:gain_row + 1, :], ones_bd)

    def fin_qa(y):
        qa_ref[...] = (rope(head_norm(y, 0)) * Q_SCALE).astype(BF16)

    def fin_ka(y):
        ka_ref[...] = rope(head_norm(y, 1)).astype(BF16)

    def fin_vat(y):
        vat = y.astype(BF16)
        for j in range(tm // MOBA_BLOCK):
            vat_ref[0, j] = vat[:, j * MOBA_BLOCK:(j + 1) * MOBA_BLOCK]

    def fin_qb(y):
        class_major(rope(head_norm(y, 2)) * Q_SCALE, qbc_ref)

    def fin_kb(y):
        kb = rope(head_norm(y, 3))
        kb_ref[...] = kb.astype(BF16)
        class_major(kb, kbc_ref)

    def fin_vb(y):
        vb_ref[...] = y.astype(BF16)
        class_major(y, vbc_ref)

    def fin_glu(ys):
        c_val, c_gate = ys
        u_ref[...] = (c_val * (1.0 / (1.0 + jnp.exp(-c_gate)))).astype(BF16)

    def fin_qm(y):
        qm_ref[...] = (head_norm(y, 4) * Q_SCALE).astype(BF16)

    pipe = _Pipe(PROJ_DEPTH)
    pipe.push(lambda: proj(0), fin_qa)
    pipe.push(lambda: proj(1), fin_ka)
    pipe.push(lambda: lax.dot_general(w_ref[:, 2 * GROUP_WIDTH:3 * GROUP_WIDTH], h, (((0,), (1,)), ((), ())),
                                      preferred_element_type=F32), fin_vat)
    pipe.push(lambda: proj(3), fin_qb)
    pipe.push(lambda: proj(4), fin_kb)
    pipe.push(lambda: proj(5), fin_vb)
    pipe.push(lambda: (proj(6), proj(7)), fin_glu)
    pipe.push(lambda: proj(8), fin_qm)
    pipe.flush()


def _layer_spec(layer, shape):
    return pl.BlockSpec((None,) + shape, lambda *_: (layer, 0, 0))


def _in_proj(x2, g, w, layer, cos, sin, gains, seq):
    rows = x2.shape[0]
    b = rows // seq
    tm = ROW_TILE
    pos_blocks = seq // tm
    nat = jax.ShapeDtypeStruct((rows, GROUP_WIDTH), BF16)
    nat_spec = pl.BlockSpec((tm, GROUP_WIDTH), lambda i: (i, 0))
    cm = jax.ShapeDtypeStruct((b, DIL_MAX, seq // DIL_MAX, GROUP_WIDTH), BF16)
    cm_spec = pl.BlockSpec((1, DIL_MAX, tm // DIL_MAX, GROUP_WIDTH),
                           lambda i: (i // pos_blocks, 0, i % pos_blocks, 0))
    vat = jax.ShapeDtypeStruct((b, seq // MOBA_BLOCK, GROUP_WIDTH, MOBA_BLOCK), BF16)
    vat_spec = pl.BlockSpec((1, tm // MOBA_BLOCK, GROUP_WIDTH, MOBA_BLOCK),
                            lambda i: (i // pos_blocks, i % pos_blocks, 0, 0))
    return pl.pallas_call(
        _in_proj_kernel,
        grid=(rows // tm,),
        in_specs=[
            pl.BlockSpec((tm, D_MODEL), lambda i: (i, 0)),
            pl.BlockSpec((1, D_MODEL), lambda i: (0, 0)),
            _layer_spec(layer, (D_MODEL, N_PROJ_GROUPS * GROUP_WIDTH)),
            pl.BlockSpec((tm, GROUP_WIDTH), lambda i: (i % pos_blocks, 0)),
            pl.BlockSpec((tm, GROUP_WIDTH), lambda i: (i % pos_blocks, 0)),
            pl.BlockSpec((8, GROUP_WIDTH), lambda i: (0, 0)),
        ],
        out_specs=[nat_spec, nat_spec, vat_spec, cm_spec, nat_spec, cm_spec, nat_spec, cm_spec, nat_spec, nat_spec],
        out_shape=[nat, nat, vat, cm, nat, cm, nat, cm, nat, nat],
        scratch_shapes=[pltpu.VMEM((GROUP_WIDTH // BAND, tm, BAND), F32)],
        compiler_params=_params(("parallel",)),
        name="in_proj",
    )(x2, g, w, cos, sin, gains)


def _moba_kernel(q_ref, k_ref, vt_ref, gout_ref, qm_ref, mem_ref, gmem_ref, wkv_ref, gk_ref, goutm_ref,
                 o_ref, om_ref):
    seq = k_ref.shape[1]
    nb = seq // MOBA_BLOCK
    tq = MOBA_BLOCK
    nrow = 2 * STAT_ROWS

    r = lax.broadcasted_iota(jnp.int32, (nrow, seq), 0)
    c = lax.broadcasted_iota(jnp.int32, (nrow, seq), 1) // MOBA_BLOCK
    avg = jnp.where(r == c, 1.0 / MOBA_BLOCK, 0.0).astype(BF16)
    kmean = jnp.dot(avg, k_ref[0], preferred_element_type=F32)
    khi = kmean.astype(BF16)
    klo = (kmean - khi.astype(F32)).astype(BF16)
    khi_st = _stack_heads(khi)
    klo_st = _stack_heads(klo)

    kk = lax.broadcasted_iota(jnp.int32, (tq, tq), 0)
    qq = lax.broadcasted_iota(jnp.int32, (tq, tq), 1)
    causal = jnp.where(kk <= qq, 0.0, NEG).astype(BF16)
    blk = lax.broadcasted_iota(jnp.int32, (nrow, tq), 0)
    gout = gout_ref[...]
    pipe = _Pipe(PIPE_DEPTH)
    kst = {}

    mh = _row_rms(mem_ref[0], gmem_ref[...]).astype(BF16)
    kv = jnp.dot(mh, wkv_ref[...], preferred_element_type=F32)
    km_st = _stack_heads(_head_rms(kv[:, :GROUP_WIDTH], gk_ref[...], _head_ones()).astype(BF16))
    vmt = kv[:, GROUP_WIDTH:].T.astype(BF16)
    n_mem = mem_ref.shape[1]
    goutm = goutm_ref[...]

    def mem_tile(qs):
        def finish(s_all):
            parts = [_softmax_pv(s_all[h * n_mem:(h + 1) * n_mem], vmt[_head_slice(h), :])
                     for h in range(GROUP_HEADS)]
            ot = jnp.concatenate([o / l for o, _, l in parts], axis=0)
            om_ref[0, qs, :] = _row_rms(ot.T, goutm).astype(BF16)

        pipe.push(lambda: _scores(km_st, qm_ref[0, qs, :]), finish)

    def stacked(n):
        if n not in kst:
            kst[n] = _stack_heads(k_ref[0, n * tq:(n + 1) * tq, :])
        return kst[n]

    for i in range(nb):
        qs = slice(i * tq, (i + 1) * tq)
        q = q_ref[0, qs, :]
        state = [None] * GROUP_HEADS

        selb = [None] * GROUP_HEADS
        if i > MOBA_TOPK:
            past = blk < i
            g_all = _dot_nt(khi_st, q) + _dot_nt(klo_st, q)
            for h in range(GROUP_HEADS):
                g = jnp.where(past, g_all[h * nrow:(h + 1) * nrow], -jnp.inf)
                rank = jnp.zeros((nrow, tq), F32)
                for mth in range(i):
                    row = g[mth:mth + 1, :]
                    rank = rank + jnp.where(row == g, jnp.where(blk > mth, 1.0, 0.0),
                                            jnp.where(row > g, 1.0, 0.0))
                selb[h] = jnp.where(rank < MOBA_TOPK, 0.0, NEG)

        def finish_own(s_all, state=state, i=i):
            for h in range(GROUP_HEADS):
                s = s_all[h * tq:(h + 1) * tq] + causal
                state[h] = _softmax_pv(s, vt_ref[0, i, _head_slice(h), :])

        def finish_past(s_all, n, state=state, selb=selb):
            for h in range(GROUP_HEADS):
                s = s_all[h * tq:(h + 1) * tq]
                acc, m_old, l_old = state[h]
                smax = _col_max(s)
                if selb[h] is None:
                    m_new = jnp.maximum(m_old, smax)
                    shift = m_new
                else:
                    bias = selb[h][n:n + 1, :]
                    m_new = jnp.maximum(m_old, smax + bias)
                    shift = m_new - bias
                alpha = jnp.exp2(m_old - m_new)
                pv, l_new = _exp_pv(s, shift, vt_ref[0, n, _head_slice(h), :])
                state[h] = (alpha * acc + pv, m_new, alpha * l_old + l_new)

        def finalize(state=state, qs=qs):
            ot = jnp.concatenate([acc / l for acc, _, l in state], axis=0)
            o_ref[0, qs, :] = _row_rms(ot.T, gout).astype(BF16)

        order = [i] + list(range(i))
        for pos, n in enumerate(order):
            score = lambda n=n, q=q: _scores(stacked(n), q)
            if n == i:
                fin = finish_own
            else:
                fin = lambda s, n=n, f=finish_past: f(s, n)
            if pos == len(order) - 1:
                fin = lambda s, fin=fin, fz=finalize: (fin(s), fz())
            pipe.push(score, fin)
        mem_tile(qs)
    pipe.flush()


def _moba(q, k, vt, gout, qm, mem, gmem, wkv, layer, gk, goutm):
    b, seq, gw = q.shape
    nb = seq // MOBA_BLOCK
    n_mem = mem.shape[1]
    nat = pl.BlockSpec((1, seq, gw), lambda bi: (bi, 0, 0))
    const = lambda shape: pl.BlockSpec(shape, lambda bi: (0, 0))
    return pl.pallas_call(
        _moba_kernel,
        grid=(b,),
        in_specs=[nat, nat, pl.BlockSpec((1, nb, gw, MOBA_BLOCK), lambda bi: (bi, 0, 0, 0)), const((1, gw)),
                  nat, pl.BlockSpec((1, n_mem, D_MODEL), lambda bi: (bi, 0, 0)), const((1, D_MODEL)),
                  _layer_spec(layer, (D_MODEL, 2 * gw)), const((1, gw)), const((1, gw))],
        out_specs=[nat, nat],
        out_shape=[jax.ShapeDtypeStruct((b, seq, gw), BF16)] * 2,
        compiler_params=_params(("parallel",)),
        name="moba",
    )(q, k, vt, gout, qm, mem, gmem, wkv, gk, goutm)


def _dilated_kernel(qc_ref, k1_ref, kc_ref, v1_ref, vc_ref, b1_ref, b4_ref, b16_ref, gout_ref, o_ref,
                    acc_ref, m_ref, l_ref, v1t_ref, fin_ref):
    seq = k1_ref.shape[1]
    per = DIL_MAX // DIL_MID
    gout = gout_ref[...]
    pipe = _Pipe(PIPE_DEPTH)

    def tile(q, k, vt, bias, done):
        tk = k.shape[0]

        def finish(s_all):
            parts = [_softmax_pv(s_all[h * tk:(h + 1) * tk] + bias, vt[_head_slice(h), :])
                     for h in range(GROUP_HEADS)]
            done(*_head_rows(parts))

        pipe.push(lambda: _scores(_stack_heads(k), q), finish)

    for c0 in range(0, seq, 2 * BAND):
        v1t_ref[:, c0:c0 + 2 * BAND] = _transpose_bf16(v1_ref[0, c0:c0 + 2 * BAND, :])

    for r in range(DIL_MAX):
        def done16(acc, m_all, l_all, r=r):
            acc_ref[r] = acc
            m_ref[r] = m_all
            l_ref[r] = l_all

        tile(qc_ref[0, r], kc_ref[0, r], _transpose_bf16(vc_ref[0, r]), b16_ref[...], done16)

    jq4 = BAND // per
    for a in range(DIL_MID):
        classes = [u * DIL_MID + a for u in range(per)]
        for jb in range(seq // DIL_MID // BAND):
            rs = slice(jb * jq4, (jb + 1) * jq4)
            if jb == 0:
                ks = rs
                bias = jnp.concatenate([b4_ref[2 * jq4 * u + jq4:2 * jq4 * (u + 1), :] for u in range(per)], axis=0)
            else:
                ks = slice((jb - 1) * jq4, (jb + 1) * jq4)
                bias = b4_ref[...]

            def done4(a1, m1, l1, classes=classes, rs=rs):
                for u, c in enumerate(classes):
                    ts = slice(u * jq4, (u + 1) * jq4)
                    acc, m_all, l_all = _merge(acc_ref[c, rs, :], m_ref[c, rs, :], l_ref[c, rs, :],
                                               a1[ts], m1[ts], l1[ts])
                    acc_ref[c, rs, :] = acc
                    m_ref[c, rs, :] = m_all
                    l_ref[c, rs, :] = l_all

            v = jnp.concatenate([vc_ref[0, c, ks, :] for c in classes], axis=0)
            tile(jnp.concatenate([qc_ref[0, c, rs, :] for c in classes], axis=0),
                 jnp.concatenate([kc_ref[0, c, ks, :] for c in classes], axis=0),
                 _transpose_bf16(v), bias, done4)

    tq = 2 * BAND
    jq1 = tq // DIL_MAX
    for jb in range(seq // tq):
        rs = slice(jb * jq1, (jb + 1) * jq1)
        if jb == 0:
            ks = slice(0, tq)
            bias = b1_ref[BAND:, :]
        else:
            ks = slice(jb * tq - BAND, (jb + 1) * tq)
            bias = b1_ref[...]

        def done1(a1, m1, l1, rs=rs, jb=jb):
            for r in range(DIL_MAX):
                ts = slice(r * jq1, (r + 1) * jq1)
                acc, _, l_all = _merge(acc_ref[r, rs, :], m_ref[r, rs, :], l_ref[r, rs, :],
                                       a1[ts], m1[ts], l1[ts])
                res = _row_rms(acc / l_all, gout)
                for half in range(GROUP_WIDTH // BAND):
                    fin_ref[half, pl.ds(r, jq1, stride=DIL_MAX), :] = res[:, half * BAND:(half + 1) * BAND]
            for half in range(GROUP_WIDTH // BAND):
                o_ref[0, jb * tq:(jb + 1) * tq, half * BAND:(half + 1) * BAND] = fin_ref[half].astype(BF16)

        tile(jnp.concatenate([qc_ref[0, r, rs, :] for r in range(DIL_MAX)], axis=0),
             k1_ref[0, ks, :], v1t_ref[:, ks], bias, done1)
    pipe.flush()


def _dilated_biases():
    def band(krel, qrel):
        dist = qrel[None, :] - krel[:, None]
        return jnp.where((dist >= 0) & (dist <= BAND), 0.0, NEG).astype(BF16)

    tq = 2 * BAND
    jq = tq // DIL_MAX
    col = jnp.arange(tq)
    b1 = band(jnp.arange(BAND + tq) - BAND, DIL_MAX * (col % jq) + col // jq)
    per = DIL_MAX // DIL_MID
    jq = BAND // per
    col = jnp.arange(BAND)
    row = jnp.arange(2 * BAND)
    b4 = band(per * (row % (2 * jq) - jq) + row // (2 * jq), per * (col % jq) + col // jq)
    b16 = band(jnp.arange(BAND), jnp.arange(BAND))
    return b1, b4, b16


def _dilated(qc, k1, kc, v1, vc, gout):
    b, seq, gw = k1.shape
    b1, b4, b16 = _dilated_biases()
    nat = pl.BlockSpec((1, seq, gw), lambda bi: (bi, 0, 0))
    cm_shape = (DIL_MAX, seq // DIL_MAX, gw)
    cm = pl.BlockSpec((1,) + cm_shape, lambda bi: (bi, 0, 0, 0))
    const = lambda shape: pl.BlockSpec(shape, lambda bi: (0, 0))
    return pl.pallas_call(
        _dilated_kernel,
        grid=(b,),
        in_specs=[cm, nat, cm, nat, cm, const(b1.shape), const(b4.shape), const(b16.shape), const((1, gw))],
        out_specs=nat,
        out_shape=jax.ShapeDtypeStruct((b, seq, gw), BF16),
        scratch_shapes=[pltpu.VMEM(cm_shape, F32)] * 3 + [pltpu.VMEM((gw, seq), BF16),
                                                          pltpu.VMEM((gw // BAND, 2 * BAND, BAND), F32)],
        compiler_params=_params(("parallel",)),
        name="dilated",
    )(qc, k1, kc, v1, vc, b1, b4, b16, gout)


def _conv_kernel(u_ref, w_ref, b_ref, lng_ref, lnb_ref, wout_ref, gout_ref, o_ref, pad_ref):
    seq = u_ref.shape[1]
    pad_ref[0:CONV_PAD, :] = jnp.zeros((CONV_PAD, GROUP_WIDTH), F32)
    pad_ref[CONV_PAD:CONV_PAD + seq, :] = u_ref[0].astype(F32)
    pad_ref[CONV_PAD + seq:, :] = jnp.zeros((16, GROUP_WIDTH), F32)
    lead = CONV_PAD - (CONV_WIDTH - 1)
    span = CONV_ROWS + 16
    for c0 in range(0, seq, CONV_ROWS):
        y = jnp.broadcast_to(b_ref[...], (CONV_ROWS, GROUP_WIDTH))
        for sub in range(8):
            z = None
            for a in range((CONV_WIDTH - sub + 7) // 8):
                j = 8 * a + sub
                t = w_ref[j:j + 1, :] * pad_ref[c0 + 8 * a:c0 + 8 * a + span, :]
                z = t if z is None else z + t
            y = y + z[lead + sub:lead + sub + CONV_ROWS]
        mu = jnp.mean(y, axis=-1, keepdims=True)
        d = y - mu
        var = jnp.mean(d * d, axis=-1, keepdims=True)
        yn = d * lax.rsqrt(var + EPS) * lng_ref[...] + lnb_ref[...]
        act = yn * (1.0 / (1.0 + jnp.exp(-yn)))
        oc = jnp.dot(act.astype(BF16), wout_ref[...], preferred_element_type=F32)
        o_ref[0, c0:c0 + CONV_ROWS, :] = _row_rms(oc, gout_ref[...]).astype(BF16)


def _conv_module(u, w, bias, lng, lnb, wout, layer, gout):
    b, seq, gw = u.shape
    const = lambda shape: pl.BlockSpec(shape, lambda bi: (0, 0))
    return pl.pallas_call(
        _conv_kernel,
        grid=(b,),
        in_specs=[pl.BlockSpec((1, seq, gw), lambda bi: (bi, 0, 0)),
                  const((CONV_PAD, gw)), const((1, gw)), const((1, gw)), const((1, gw)),
                  _layer_spec(layer, (gw, gw)), const((1, gw))],
        out_specs=pl.BlockSpec((1, seq, gw), lambda bi: (bi, 0, 0)),
        out_shape=jax.ShapeDtypeStruct((b, seq, gw), BF16),
        scratch_shapes=[pltpu.VMEM((CONV_PAD + seq + 16, gw), F32)],
        compiler_params=_params(("parallel",)),
        name="conv_module",
    )(u, w, bias, lng, lnb, wout, gout)


def _ffn_kernel(pos_blocks, x_ref, xh_ref, oa_ref, ob_ref, oc_ref, om_ref, oah_ref, obh_ref, och_ref, omh_ref,
                wout_ref, g_ref, wup_ref, cw_ref, cb_ref, wdown_ref, o_ref, mix_ref, h_ref, act_ref):
    tm = x_ref.shape[0]
    groups = ((oa_ref, oah_ref), (ob_ref, obh_ref), (oc_ref, och_ref), (om_ref, omh_ref))
    for g, (main_ref, halo_ref) in enumerate(groups):
        cs = slice(g * GROUP_WIDTH, (g + 1) * GROUP_WIDTH)
        mix_ref[0:HALO, cs] = halo_ref[...]
        mix_ref[HALO:, cs] = main_ref[...]
    upd = jnp.dot(mix_ref[...], wout_ref[...], preferred_element_type=F32)
    x = x_ref[...] + upd[HALO:]
    xh = xh_ref[...] + upd[0:HALO]
    keep = jnp.where(pl.program_id(0) % pos_blocks == 0, 0.0, 1.0)
    h_ref[0:HALO, :] = (_row_rms(xh, g_ref[...]) * keep).astype(BF16)
    h_ref[HALO:, :] = _row_rms(x, g_ref[...]).astype(BF16)
    h = h_ref[...]

    def conv(col0):
        cs = slice(col0, col0 + FFN_CHUNK)
        u = jnp.dot(h, wup_ref[:, cs], preferred_element_type=F32)
        y = cb_ref[:, cs] + cw_ref[2:3, cs] * u[HALO:HALO + tm]
        y = y + cw_ref[1:2, cs] * u[HALO - 1:HALO - 1 + tm]
        return y + cw_ref[0:1, cs] * u[HALO - 2:HALO - 2 + tm]

    for c in range(D_FF // FFN_CHUNK):
        gate = conv(c * FFN_CHUNK)
        val = conv(D_FF + c * FFN_CHUNK)
        act = gate * (1.0 / (1.0 + jnp.exp(-gate))) * val
        act_ref[:, c * FFN_CHUNK:(c + 1) * FFN_CHUNK] = act.astype(BF16)
    o_ref[...] = x + jnp.dot(act_ref[...], wdown_ref[...], preferred_element_type=F32)


def _ffn(x2, mixes, wout, g, wup, cw, cb, wdown, layer, seq):
    rows = x2.shape[0]
    tm = ROW_TILE
    pos_blocks = seq // tm
    hb = tm // HALO
    const = lambda shape: pl.BlockSpec(shape, lambda i: (0, 0), pipeline_mode=pl.Buffered(1))
    weight = lambda shape: pl.BlockSpec((None,) + shape, lambda i: (layer, 0, 0), pipeline_mode=pl.Buffered(1))
    halo = lambda i: (jnp.maximum(i * hb - 1, 0), 0)
    xspec = pl.BlockSpec((tm, D_MODEL), lambda i: (i, 0))
    return pl.pallas_call(
        functools.partial(_ffn_kernel, pos_blocks),
        grid=(rows // tm,),
        in_specs=[xspec, pl.BlockSpec((HALO, D_MODEL), halo)]
                 + [pl.BlockSpec((tm, GROUP_WIDTH), lambda i: (i, 0))] * len(mixes)
                 + [pl.BlockSpec((HALO, GROUP_WIDTH), halo)] * len(mixes)
                 + [weight((D_MODEL, D_MODEL)), const((1, D_MODEL)), weight((D_MODEL, 2 * D_FF)),
                    const((8, 2 * D_FF)), const((1, 2 * D_FF)), weight((D_FF, D_MODEL))],
        out_specs=xspec,
        out_shape=jax.ShapeDtypeStruct((rows, D_MODEL), F32),
        scratch_shapes=[pltpu.VMEM((HALO + tm, D_MODEL), BF16), pltpu.VMEM((HALO + tm, D_MODEL), BF16),
                        pltpu.VMEM((tm, D_FF), BF16)],
        compiler_params=_params(("parallel",)),
        name="ffn",
    )(x2, x2, *mixes, *mixes, wout, g, wup, cw, cb, wdown)


def _rope_tables(seq):
    inv = ROPE_THETA ** (-jnp.arange(0, HEAD_DIM, 2, dtype=F32) / HEAD_DIM)
    ang = jnp.arange(seq, dtype=F32)[:, None] * inv[None, :]
    ang = jnp.concatenate([ang, ang], axis=-1)
    sign = jnp.where(jnp.arange(HEAD_DIM) < HEAD_DIM // 2, -1.0, 1.0).astype(F32)
    cos = jnp.tile(jnp.cos(ang), (1, GROUP_HEADS))
    sin = jnp.tile(jnp.sin(ang) * sign[None, :], (1, GROUP_HEADS))
    return cos, sin


def _pad_rows(a, rows):
    return jnp.concatenate([a, jnp.zeros((rows - a.shape[0],) + a.shape[1:], a.dtype)], axis=0)


def kernel(x, mem, norm_mix, w_in, q_norm_a, k_norm_a, q_norm_b, k_norm_b, q_norm_m, k_norm_m,
           mem_norm, w_mem_kv, conv_w, conv_b, conv_ln_g, conv_ln_b, w_conv_out, out_norm, w_out,
           norm_ffn, w_up, ffn_conv_w, ffn_conv_b, w_down):
    b, seq, d = x.shape
    depth = w_in.shape[0]
    gw = GROUP_WIDTH
    cos, sin = _rope_tables(seq)
    x2 = x.reshape(b * seq, d)
    tile_h = lambda g: jnp.tile(g, GROUP_HEADS)
    r3 = lambda t: t.reshape(b, seq, gw)
    f2 = lambda t: t.reshape(b * seq, gw)
    w_in_b, w_mem_b, w_co_b = w_in.astype(BF16), w_mem_kv.astype(BF16), w_conv_out.astype(BF16)
    w_out_b, w_up_b, w_down_b = w_out.astype(BF16), w_up.astype(BF16), w_down.astype(BF16)
    for i in range(depth):
        gains = _pad_rows(jnp.stack([tile_h(q_norm_a[i]), tile_h(k_norm_a[i]), tile_h(q_norm_b[i]),
                                     tile_h(k_norm_b[i]), tile_h(q_norm_m[i])]), 8)
        gout = out_norm[i].reshape(4, 1, gw)
        qa, ka, vat, qbc, kb, kbc, vb, vbc, u, qm = _in_proj(x2, norm_mix[i][None], w_in_b, i, cos, sin, gains, seq)
        oa, om = _moba(r3(qa), r3(ka), vat, gout[0], r3(qm), mem, mem_norm[i][None], w_mem_b, i,
                       tile_h(k_norm_m[i])[None], gout[3])
        ob = _dilated(qbc, r3(kb), kbc, r3(vb), vbc, gout[1])
        oc = _conv_module(r3(u), _pad_rows(conv_w[i], CONV_PAD), conv_b[i][None], conv_ln_g[i][None],
                          conv_ln_b[i][None], w_co_b, i, gout[2])
        x2 = _ffn(x2, [f2(oa), f2(ob), f2(oc), f2(om)], w_out_b, norm_ffn[i][None], w_up_b,
                  _pad_rows(ffn_conv_w[i], 8), ffn_conv_b[i][None], w_down_b, i, seq)
    return x2.reshape(b, seq, d)
```

```python
import functools

import jax
import jax.numpy as jnp
from jax import lax
from jax.experimental import pallas as pl
from jax.experimental.pallas import tpu as pltpu

F32 = jnp.float32
BF16 = jnp.bfloat16

D_MODEL = 1024
HEAD_DIM = 64
GROUP_WIDTH = 256
GROUP_HEADS = 4
N_PROJ_GROUPS = 9
MOBA_BLOCK = 256
MOBA_TOPK = 3
BAND = 128
DIL_MID = 4
DIL_MAX = 16
CONV_WIDTH = 31
FFN_CONV_WIDTH = 3
D_FF = 2816
ROPE_THETA = 10000.0
EPS = 1e-6
ATTN_SCALE = HEAD_DIM ** -0.5
Q_SCALE = ATTN_SCALE * 1.4426950408889634
NEG = -1e30

ROW_TILE = 1024
FFN_CHUNK = 256
HALO = 16
CONV_PAD = 32
CONV_ROWS = 1024
PIPE_DEPTH = 3
MOBA_DEPTH = 2
PROJ_DEPTH = 1
STAT_ROWS = 8
VMEM_LIMIT = 56 * 1024 * 1024


def _params(sem):
    return pltpu.CompilerParams(dimension_semantics=sem, vmem_limit_bytes=VMEM_LIMIT)


def _lane_head(shape):
    return lax.broadcasted_iota(jnp.int32, shape, 1) // HEAD_DIM


def _head_ones():
    r = lax.broadcasted_iota(jnp.int32, (GROUP_WIDTH, GROUP_WIDTH), 0) // HEAD_DIM
    c = lax.broadcasted_iota(jnp.int32, (GROUP_WIDTH, GROUP_WIDTH), 1) // HEAD_DIM
    return jnp.where(r == c, 1.0, 0.0).astype(BF16)


def _head_rms(p, gain, ones_bd):
    ss = jnp.dot((p * p).astype(BF16), ones_bd, preferred_element_type=F32)
    return p * lax.rsqrt(ss * (1.0 / HEAD_DIM) + EPS) * gain


def _row_rms(x, gain):
    return x * lax.rsqrt(jnp.mean(x * x, axis=-1, keepdims=True) + EPS) * gain


def _dot_nt(a, b):
    return lax.dot_general(a, b, (((1,), (1,)), ((), ())), preferred_element_type=F32)


def _stack_heads(k):
    lh = _lane_head((1, GROUP_WIDTH))
    return jnp.concatenate([k * jnp.where(lh == h, 1.0, 0.0).astype(BF16) for h in range(GROUP_HEADS)], axis=0)


def _head_slice(h):
    return slice(h * HEAD_DIM, (h + 1) * HEAD_DIM)


def _transpose_bf16(v):
    return v.T


class _Pipe:
    def __init__(self, depth=3):
        self.depth = depth
        self.items = []

    def push(self, score, finish):
        self.items.append((score(), finish))
        if len(self.items) > self.depth:
            s, f = self.items.pop(0)
            f(s)

    def flush(self):
        for s, f in self.items:
            f(s)
        self.items = []


def _scores(k_st, q):
    return _dot_nt(k_st, q).astype(BF16)


def _col_max(s):
    tk, tq = s.shape
    part = jnp.max(s.reshape(tk // 16, 16, tq), axis=0)
    return jnp.max(part.astype(F32), axis=0, keepdims=True)


def _exp_pv(s, shift, vt_h):
    p = jnp.exp2(s - shift.astype(BF16))
    ones = jnp.ones((2 * STAT_ROWS, vt_h.shape[1]), BF16)
    pv = jnp.dot(jnp.concatenate([vt_h, ones], axis=0), p, preferred_element_type=F32)
    return pv[:HEAD_DIM], pv[HEAD_DIM:HEAD_DIM + 1]


def _softmax_pv(s, vt_h):
    m = _col_max(s)
    o, l = _exp_pv(s, m, vt_h)
    return o, m, l


def _head_rows(parts):
    tq = parts[0][0].shape[1]
    rows = lambda xs: jnp.concatenate(xs, axis=0).T
    return (rows([o for o, _, _ in parts]),
            rows([jnp.broadcast_to(m, (HEAD_DIM, tq)) for _, m, _ in parts]),
            rows([jnp.broadcast_to(l, (HEAD_DIM, tq)) for _, _, l in parts]))


def _merge(acc0, m0, l0, acc1, m1, l1):
    mn = jnp.maximum(m0, m1)
    a0 = jnp.exp2(m0 - mn)
    a1 = jnp.exp2(m1 - mn)
    return acc0 * a0 + acc1 * a1, mn, l0 * a0 + l1 * a1


def _in_proj_kernel(x_ref, g_ref, w_ref, cos_ref, sin_ref, gains_ref,
                    qa_ref, ka_ref, vat_ref, qbc_ref, kb_ref, kbc_ref, vb_ref, vbc_ref, u_ref, qm_ref,
                    perm_ref):
    h = _row_rms(x_ref[...], g_ref[...]).astype(BF16)
    ones_bd = _head_ones()
    cos = cos_ref[...]
    sin = sin_ref[...]
    tm = h.shape[0]
    lower = (lax.broadcasted_iota(jnp.int32, (tm, GROUP_WIDTH), 1) % HEAD_DIM) < (HEAD_DIM // 2)

    def proj(j):
        return jnp.dot(h, w_ref[:, j * GROUP_WIDTH:(j + 1) * GROUP_WIDTH], preferred_element_type=F32)

    def rope(y):
        rot = jnp.where(lower, pltpu.roll(y, GROUP_WIDTH - HEAD_DIM // 2, 1), pltpu.roll(y, HEAD_DIM // 2, 1))
        return y * cos + rot * sin

    def class_major(y, out_ref):
        for half in range(GROUP_WIDTH // BAND):
            ls = slice(half * BAND, (half + 1) * BAND)
            perm_ref[half] = y[:, ls]
            for r in range(DIL_MAX):
                out_ref[0, r, :, ls] = perm_ref[half, pl.ds(r, tm // DIL_MAX, stride=DIL_MAX), :].astype(BF16)

    def head_norm(y, gain_row):
        return _head_rms(y, gains_ref[gain_row:gain_row + 1, :], ones_bd)

    def fin_qa(y):
        qa_ref[...] = (rope(head_norm(y, 0)) * Q_SCALE).astype(BF16)

    def fin_ka(y):
        ka_ref[...] = rope(head_norm(y, 1)).astype(BF16)

    def fin_vat(y):
        vat = y.astype(BF16)
        for j in range(tm // MOBA_BLOCK):
            vat_ref[0, j] = vat[:, j * MOBA_BLOCK:(j + 1) * MOBA_BLOCK]

    def fin_qb(y):
        class_major(rope(head_norm(y, 2)) * Q_SCALE, qbc_ref)

    def fin_kb(y):
        kb = rope(head_norm(y, 3))
        kb_ref[...] = kb.astype(BF16)
        class_major(kb, kbc_ref)

    def fin_vb(y):
        vb_ref[...] = y.astype(BF16)
        class_major(y, vbc_ref)

    def fin_glu(ys):
        c_val, c_gate = ys
        u_ref[...] = (c_val * (1.0 / (1.0 + jnp.exp(-c_gate)))).astype(BF16)

    def fin_qm(y):
        qm_ref[...] = (head_norm(y, 4) * Q_SCALE).astype(BF16)

    pipe = _Pipe(PROJ_DEPTH)
    pipe.push(lambda: proj(0), fin_qa)
    pipe.push(lambda: proj(1), fin_ka)
    pipe.push(lambda: lax.dot_general(w_ref[:, 2 * GROUP_WIDTH:3 * GROUP_WIDTH], h, (((0,), (1,)), ((), ())),
                                      preferred_element_type=F32), fin_vat)
    pipe.push(lambda: proj(3), fin_qb)
    pipe.push(lambda: proj(4), fin_kb)
    pipe.push(lambda: proj(5), fin_vb)
    pipe.push(lambda: (proj(6), proj(7)), fin_glu)
    pipe.push(lambda: proj(8), fin_qm)
    pipe.flush()


def _layer_spec(layer, shape):
    return pl.BlockSpec((None,) + shape, lambda *_: (layer, 0, 0))


def _in_proj(x2, g, w, layer, cos, sin, gains, seq):
    rows = x2.shape[0]
    b = rows // seq
    tm = ROW_TILE
    pos_blocks = seq // tm
    nat = jax.ShapeDtypeStruct((rows, GROUP_WIDTH), BF16)
    nat_spec = pl.BlockSpec((tm, GROUP_WIDTH), lambda i: (i, 0))
    cm = jax.ShapeDtypeStruct((b, DIL_MAX, seq // DIL_MAX, GROUP_WIDTH), BF16)
    cm_spec = pl.BlockSpec((1, DIL_MAX, tm // DIL_MAX, GROUP_WIDTH),
                           lambda i: (i // pos_blocks, 0, i % pos_blocks, 0))
    vat = jax.ShapeDtypeStruct((b, seq // MOBA_BLOCK, GROUP_WIDTH, MOBA_BLOCK), BF16)
    vat_spec = pl.BlockSpec((1, tm // MOBA_BLOCK, GROUP_WIDTH, MOBA_BLOCK),
                            lambda i: (i // pos_blocks, i % pos_blocks, 0, 0))
    return pl.pallas_call(
        _in_proj_kernel,
        grid=(rows // tm,),
        in_specs=[
            pl.BlockSpec((tm, D_MODEL), lambda i: (i, 0)),
            pl.BlockSpec((1, D_MODEL), lambda i: (0, 0)),
            _layer_spec(layer, (D_MODEL, N_PROJ_GROUPS * GROUP_WIDTH)),
            pl.BlockSpec((tm, GROUP_WIDTH), lambda i: (i % pos_blocks, 0)),
            pl.BlockSpec((tm, GROUP_WIDTH), lambda i: (i % pos_blocks, 0)),
            pl.BlockSpec((8, GROUP_WIDTH), lambda i: (0, 0)),
        ],
        out_specs=[nat_spec, nat_spec, vat_spec, cm_spec, nat_spec, cm_spec, nat_spec, cm_spec, nat_spec, nat_spec],
        out_shape=[nat, nat, vat, cm, nat, cm, nat, cm, nat, nat],
        scratch_shapes=[pltpu.VMEM((GROUP_WIDTH // BAND, tm, BAND), F32)],
        compiler_params=_params(("parallel",)),
        name="in_proj",
    )(x2, g, w, cos, sin, gains)


def _moba_kernel(q_ref, k_ref, vt_ref, gout_ref, qm_ref, mem_ref, gmem_ref, wkv_ref, gk_ref, goutm_ref,
                 o_ref, om_ref):
    seq = k_ref.shape[1]
    nb = seq // MOBA_BLOCK
    tq = MOBA_BLOCK
    nrow = 2 * STAT_ROWS

    r = lax.broadcasted_iota(jnp.int32, (nrow, seq), 0)
    c = lax.broadcasted_iota(jnp.int32, (nrow, seq), 1) // MOBA_BLOCK
    avg = jnp.where(r == c, 1.0 / MOBA_BLOCK, 0.0).astype(BF16)
    kmean = jnp.dot(avg, k_ref[0], preferred_element_type=F32)
    khi = kmean.astype(BF16)
    klo = (kmean - khi.astype(F32)).astype(BF16)
    khi_st = _stack_heads(khi)
    klo_st = _stack_heads(klo)

    kk = lax.broadcasted_iota(jnp.int32, (tq, tq), 0)
    qq = lax.broadcasted_iota(jnp.int32, (tq, tq), 1)
    causal = jnp.where(kk <= qq, 0.0, NEG).astype(BF16)
    blk = lax.broadcasted_iota(jnp.int32, (nrow, tq), 0)
    gout = gout_ref[...]
    pipe = _Pipe(MOBA_DEPTH)
    kst = {}

    mh = _row_rms(mem_ref[0], gmem_ref[...]).astype(BF16)
    kv = jnp.dot(mh, wkv_ref[...], preferred_element_type=F32)
    km_st = _stack_heads(_head_rms(kv[:, :GROUP_WIDTH], gk_ref[...], _head_ones()).astype(BF16))
    vmt = kv[:, GROUP_WIDTH:].T.astype(BF16)
    n_mem = mem_ref.shape[1]
    goutm = goutm_ref[...]

    def mem_tile(qs):
        def finish(s_all):
            parts = [_softmax_pv(s_all[h * n_mem:(h + 1) * n_mem], vmt[_head_slice(h), :])
                     for h in range(GROUP_HEADS)]
            ot = jnp.concatenate([o / l for o, _, l in parts], axis=0)
            om_ref[0, qs, :] = _row_rms(ot.T, goutm).astype(BF16)

        pipe.push(lambda: _scores(km_st, qm_ref[0, qs, :]), finish)

    def stacked(n):
        if n not in kst:
            kst[n] = _stack_heads(k_ref[0, n * tq:(n + 1) * tq, :])
        return kst[n]

    for i in range(nb):
        qs = slice(i * tq, (i + 1) * tq)
        q = q_ref[0, qs, :]
        state = [None] * GROUP_HEADS

        selb = [None] * GROUP_HEADS
        if i > MOBA_TOPK:
            past = blk < i
            g_all = _dot_nt(khi_st, q) + _dot_nt(klo_st, q)
            for h in range(GROUP_HEADS):
                g = jnp.where(past, g_all[h * nrow:(h + 1) * nrow], -jnp.inf)
                rank = jnp.zeros((nrow, tq), F32)
                for mth in range(i):
                    row = g[mth:mth + 1, :]
                    rank = rank + jnp.where(row == g, jnp.where(blk > mth, 1.0, 0.0),
                                            jnp.where(row > g, 1.0, 0.0))
                selb[h] = jnp.where(rank < MOBA_TOPK, 0.0, NEG)

        def finish_own(s_all, state=state, i=i):
            for h in range(GROUP_HEADS):
                s = s_all[h * tq:(h + 1) * tq] + causal
                state[h] = _softmax_pv(s, vt_ref[0, i, _head_slice(h), :])

        def finish_past(s_all, n, state=state, selb=selb):
            for h in range(GROUP_HEADS):
                s = s_all[h * tq:(h + 1) * tq]
                acc, m_old, l_old = state[h]
                smax = _col_max(s)
                if selb[h] is None:
                    m_new = jnp.maximum(m_old, smax)
                    shift = m_new
                else:
                    bias = selb[h][n:n + 1, :]
                    m_new = jnp.maximum(m_old, smax + bias)
                    shift = m_new - bias
                alpha = jnp.exp2(m_old - m_new)
                pv, l_new = _exp_pv(s, shift, vt_ref[0, n, _head_slice(h), :])
                state[h] = (alpha * acc + pv, m_new, alpha * l_old + l_new)

        def finalize(state=state, qs=qs):
            ot = jnp.concatenate([acc / l for acc, _, l in state], axis=0)
            o_ref[0, qs, :] = _row_rms(ot.T, gout).astype(BF16)

        order = [i] + list(range(i))
        for pos, n in enumerate(order):
            score = lambda n=n, q=q: _scores(stacked(n), q)
            if n == i:
                fin = finish_own
            else:
                fin = lambda s, n=n, f=finish_past: f(s, n)
            if pos == len(order) - 1:
                fin = lambda s, fin=fin, fz=finalize: (fin(s), fz())
            pipe.push(score, fin)
        mem_tile(qs)
    pipe.flush()


def _moba(q, k, vt, gout, qm, mem, gmem, wkv, layer, gk, goutm):
    b, seq, gw = q.shape
    nb = seq // MOBA_BLOCK
    n_mem = mem.shape[1]
    nat = pl.BlockSpec((1, seq, gw), lambda bi: (bi, 0, 0))
    const = lambda shape: pl.BlockSpec(shape, lambda bi: (0, 0))
    return pl.pallas_call(
        _moba_kernel,
        grid=(b,),
        in_specs=[nat, nat, pl.BlockSpec((1, nb, gw, MOBA_BLOCK), lambda bi: (bi, 0, 0, 0)), const((1, gw)),
                  nat, pl.BlockSpec((1, n_mem, D_MODEL), lambda bi: (bi, 0, 0)), const((1, D_MODEL)),
                  _layer_spec(layer, (D_MODEL, 2 * gw)), const((1, gw)), const((1, gw))],
        out_specs=[nat, nat],
        out_shape=[jax.ShapeDtypeStruct((b, seq, gw), BF16)] * 2,
        compiler_params=_params(("parallel",)),
        name="moba",
    )(q, k, vt, gout, qm, mem, gmem, wkv, gk, goutm)


def _dilated_kernel(qc_ref, k1_ref, kc_ref, v1_ref, vc_ref, b1_ref, b4_ref, b16_ref, gout_ref, o_ref,
                    acc_ref, m_ref, l_ref, v1t_ref, fin_ref):
    seq = k1_ref.shape[1]
    per = DIL_MAX // DIL_MID
    gout = gout_ref[...]
    pipe = _Pipe(PIPE_DEPTH)

    def tile(q, k, vt, bias, done):
        tk = k.shape[0]

        def finish(s_all):
            parts = [_softmax_pv(s_all[h * tk:(h + 1) * tk] + bias, vt[_head_slice(h), :])
                     for h in range(GROUP_HEADS)]
            done(*_head_rows(parts))

        pipe.push(lambda: _scores(_stack_heads(k), q), finish)

    for c0 in range(0, seq, 2 * BAND):
        v1t_ref[:, c0:c0 + 2 * BAND] = _transpose_bf16(v1_ref[0, c0:c0 + 2 * BAND, :])

    for r in range(DIL_MAX):
        def done16(acc, m_all, l_all, r=r):
            acc_ref[r] = acc
            m_ref[r] = m_all
            l_ref[r] = l_all

        tile(qc_ref[0, r], kc_ref[0, r], _transpose_bf16(vc_ref[0, r]), b16_ref[...], done16)

    jq4 = BAND // per

    def tile4(a, jb):
        classes = [u * DIL_MID + a for u in range(per)]
        rs = slice(jb * jq4, (jb + 1) * jq4)
        if jb == 0:
            ks = rs
            bias = jnp.concatenate([b4_ref[2 * jq4 * u + jq4:2 * jq4 * (u + 1), :] for u in range(per)], axis=0)
        else:
            ks = slice((jb - 1) * jq4, (jb + 1) * jq4)
            bias = b4_ref[...]

        def done4(a1, m1, l1):
            for u, c in enumerate(classes):
                ts = slice(u * jq4, (u + 1) * jq4)
                acc, m_all, l_all = _merge(acc_ref[c, rs, :], m_ref[c, rs, :], l_ref[c, rs, :],
                                           a1[ts], m1[ts], l1[ts])
                acc_ref[c, rs, :] = acc
                m_ref[c, rs, :] = m_all
                l_ref[c, rs, :] = l_all

        v = jnp.concatenate([vc_ref[0, c, ks, :] for c in classes], axis=0)
        tile(jnp.concatenate([qc_ref[0, c, rs, :] for c in classes], axis=0),
             jnp.concatenate([kc_ref[0, c, ks, :] for c in classes], axis=0),
             _transpose_bf16(v), bias, done4)

    tq = 2 * BAND
    jq1 = tq // DIL_MAX

    def tile1(jb):
        rs = slice(jb * jq1, (jb + 1) * jq1)
        if jb == 0:
            ks = slice(0, tq)
            bias = b1_ref[BAND:, :]
        else:
            ks = slice(jb * tq - BAND, (jb + 1) * tq)
            bias = b1_ref[...]

        def done1(a1, m1, l1):
            for r in range(DIL_MAX):
                ts = slice(r * jq1, (r + 1) * jq1)
                acc, _, l_all = _merge(acc_ref[r, rs, :], m_ref[r, rs, :], l_ref[r, rs, :],
                                       a1[ts], m1[ts], l1[ts])
                res = _row_rms(acc / l_all, gout)
                for half in range(GROUP_WIDTH // BAND):
                    fin_ref[half, pl.ds(r, jq1, stride=DIL_MAX), :] = res[:, half * BAND:(half + 1) * BAND]
            for half in range(GROUP_WIDTH // BAND):
                o_ref[0, jb * tq:(jb + 1) * tq, half * BAND:(half + 1) * BAND] = fin_ref[half].astype(BF16)

        tile(jnp.concatenate([qc_ref[0, r, rs, :] for r in range(DIL_MAX)], axis=0),
             k1_ref[0, ks, :], v1t_ref[:, ks], bias, done1)

    for jb4 in range(seq // DIL_MID // BAND):
        for a in range(DIL_MID):
            tile4(a, jb4)
        for jb in range(jb4 * jq4 // jq1, (jb4 + 1) * jq4 // jq1):
            tile1(jb)
    pipe.flush()


def _dilated_biases():
    def band(krel, qrel):
        dist = qrel[None, :] - krel[:, None]
        return jnp.where((dist >= 0) & (dist <= BAND), 0.0, NEG).astype(BF16)

    tq = 2 * BAND
    jq = tq // DIL_MAX
    col = jnp.arange(tq)
    b1 = band(jnp.arange(BAND + tq) - BAND, DIL_MAX * (col % jq) + col // jq)
    per = DIL_MAX // DIL_MID
    jq = BAND // per
    col = jnp.arange(BAND)
    row = jnp.arange(2 * BAND)
    b4 = band(per * (row % (2 * jq) - jq) + row // (2 * jq), per * (col % jq) + col // jq)
    b16 = band(jnp.arange(BAND), jnp.arange(BAND))
    return b1, b4, b16


def _dilated(qc, k1, kc, v1, vc, gout):
    b, seq, gw = k1.shape
    b1, b4, b16 = _dilated_biases()
    nat = pl.BlockSpec((1, seq, gw), lambda bi: (bi, 0, 0))
    cm_shape = (DIL_MAX, seq // DIL_MAX, gw)
    cm = pl.BlockSpec((1,) + cm_shape, lambda bi: (bi, 0, 0, 0))
    const = lambda shape: pl.BlockSpec(shape, lambda bi: (0, 0))
    return pl.pallas_call(
        _dilated_kernel,
        grid=(b,),
        in_specs=[cm, nat, cm, nat, cm, const(b1.shape), const(b4.shape), const(b16.shape), const((1, gw))],
        out_specs=nat,
        out_shape=jax.ShapeDtypeStruct((b, seq, gw), BF16),
        scratch_shapes=[pltpu.VMEM(cm_shape, F32)] * 3 + [pltpu.VMEM((gw, seq), BF16),
                                                          pltpu.VMEM((gw // BAND, 2 * BAND, BAND), F32)],
        compiler_params=_params(("parallel",)),
        name="dilated",
    )(qc, k1, kc, v1, vc, b1, b4, b16, gout)


def _conv_kernel(u_ref, w_ref, b_ref, lng_ref, lnb_ref, wout_ref, gout_ref, o_ref, pad_ref):
    seq = u_ref.shape[1]
    pad_ref[0:CONV_PAD, :] = jnp.zeros((CONV_PAD, GROUP_WIDTH), F32)
    pad_ref[CONV_PAD:CONV_PAD + seq, :] = u_ref[0].astype(F32)
    pad_ref[CONV_PAD + seq:, :] = jnp.zeros((16, GROUP_WIDTH), F32)
    lead = CONV_PAD - (CONV_WIDTH - 1)
    span = CONV_ROWS + 16
    for c0 in range(0, seq, CONV_ROWS):
        y = jnp.broadcast_to(b_ref[...], (CONV_ROWS, GROUP_WIDTH))
        for sub in range(8):
            z = None
            for a in range((CONV_WIDTH - sub + 7) // 8):
                j = 8 * a + sub
                t = w_ref[j:j + 1, :] * pad_ref[c0 + 8 * a:c0 + 8 * a + span, :]
                z = t if z is None else z + t
            y = y + z[lead + sub:lead + sub + CONV_ROWS]
        mu = jnp.mean(y, axis=-1, keepdims=True)
        d = y - mu
        var = jnp.mean(d * d, axis=-1, keepdims=True)
        yn = d * lax.rsqrt(var + EPS) * lng_ref[...] + lnb_ref[...]
        act = yn * (1.0 / (1.0 + jnp.exp(-yn)))
        oc = jnp.dot(act.astype(BF16), wout_ref[...], preferred_element_type=F32)
        o_ref[0, c0:c0 + CONV_ROWS, :] = _row_rms(oc, gout_ref[...]).astype(BF16)


def _conv_module(u, w, bias, lng, lnb, wout, layer, gout):
    b, seq, gw = u.shape
    const = lambda shape: pl.BlockSpec(shape, lambda bi: (0, 0))
    return pl.pallas_call(
        _conv_kernel,
        grid=(b,),
        in_specs=[pl.BlockSpec((1, seq, gw), lambda bi: (bi, 0, 0)),
                  const((CONV_PAD, gw)), const((1, gw)), const((1, gw)), const((1, gw)),
                  _layer_spec(layer, (gw, gw)), const((1, gw))],
        out_specs=pl.BlockSpec((1, seq, gw), lambda bi: (bi, 0, 0)),
        out_shape=jax.ShapeDtypeStruct((b, seq, gw), BF16),
        scratch_shapes=[pltpu.VMEM((CONV_PAD + seq + 16, gw), F32)],
        compiler_params=_params(("parallel",)),
        name="conv_module",
    )(u, w, bias, lng, lnb, wout, gout)


def _ffn_kernel(pos_blocks, x_ref, xh_ref, oa_ref, ob_ref, oc_ref, om_ref, oah_ref, obh_ref, och_ref, omh_ref,
                wout_ref, g_ref, wup_ref, cw_ref, cb_ref, wdown_ref, o_ref, mix_ref, h_ref, act_ref):
    tm = x_ref.shape[0]
    groups = ((oa_ref, oah_ref), (ob_ref, obh_ref), (oc_ref, och_ref), (om_ref, omh_ref))
    for g, (main_ref, halo_ref) in enumerate(groups):
        cs = slice(g * GROUP_WIDTH, (g + 1) * GROUP_WIDTH)
        mix_ref[0:HALO, cs] = halo_ref[...]
        mix_ref[HALO:, cs] = main_ref[...]
    upd = jnp.dot(mix_ref[...], wout_ref[...], preferred_element_type=F32)
    x = x_ref[...] + upd[HALO:]
    xh = xh_ref[...] + upd[0:HALO]
    keep = jnp.where(pl.program_id(0) % pos_blocks == 0, 0.0, 1.0)
    h_ref[0:HALO, :] = (_row_rms(xh, g_ref[...]) * keep).astype(BF16)
    h_ref[HALO:, :] = _row_rms(x, g_ref[...]).astype(BF16)
    h = h_ref[...]

    def conv(col0):
        cs = slice(col0, col0 + FFN_CHUNK)
        u = jnp.dot(h, wup_ref[:, cs], preferred_element_type=F32)
        y = cb_ref[:, cs] + cw_ref[2:3, cs] * u[HALO:HALO + tm]
        y = y + cw_ref[1:2, cs] * u[HALO - 1:HALO - 1 + tm]
        return y + cw_ref[0:1, cs] * u[HALO - 2:HALO - 2 + tm]

    for c in range(D_FF // FFN_CHUNK):
        gate = conv(c * FFN_CHUNK)
        val = conv(D_FF + c * FFN_CHUNK)
        act = gate * (1.0 / (1.0 + jnp.exp(-gate))) * val
        act_ref[:, c * FFN_CHUNK:(c + 1) * FFN_CHUNK] = act.astype(BF16)
    o_ref[...] = x + jnp.dot(act_ref[...], wdown_ref[...], preferred_element_type=F32)


def _ffn(x2, mixes, wout, g, wup, cw, cb, wdown, layer, seq):
    rows = x2.shape[0]
    tm = ROW_TILE
    pos_blocks = seq // tm
    hb = tm // HALO
    const = lambda shape: pl.BlockSpec(shape, lambda i: (0, 0), pipeline_mode=pl.Buffered(1))
    weight = lambda shape: pl.BlockSpec((None,) + shape, lambda i: (layer, 0, 0), pipeline_mode=pl.Buffered(1))
    halo = lambda i: (jnp.maximum(i * hb - 1, 0), 0)
    xspec = pl.BlockSpec((tm, D_MODEL), lambda i: (i, 0))
    return pl.pallas_call(
        functools.partial(_ffn_kernel, pos_blocks),
        grid=(rows // tm,),
        in_specs=[xspec, pl.BlockSpec((HALO, D_MODEL), halo)]
                 + [pl.BlockSpec((tm, GROUP_WIDTH), lambda i: (i, 0))] * len(mixes)
                 + [pl.BlockSpec((HALO, GROUP_WIDTH), halo)] * len(mixes)
                 + [weight((D_MODEL, D_MODEL)), const((1, D_MODEL)), weight((D_MODEL, 2 * D_FF)),
                    const((8, 2 * D_FF)), const((1, 2 * D_FF)), weight((D_FF, D_MODEL))],
        out_specs=xspec,
        out_shape=jax.ShapeDtypeStruct((rows, D_MODEL), F32),
        scratch_shapes=[pltpu.VMEM((HALO + tm, D_MODEL), BF16), pltpu.VMEM((HALO + tm, D_MODEL), BF16),
                        pltpu.VMEM((tm, D_FF), BF16)],
        compiler_params=_params(("parallel",)),
        name="ffn",
    )(x2, x2, *mixes, *mixes, wout, g, wup, cw, cb, wdown)


def _rope_tables(seq):
    inv = ROPE_THETA ** (-jnp.arange(0, HEAD_DIM, 2, dtype=F32) / HEAD_DIM)
    ang = jnp.arange(seq, dtype=F32)[:, None] * inv[None, :]
    ang = jnp.concatenate([ang, ang], axis=-1)
    sign = jnp.where(jnp.arange(HEAD_DIM) < HEAD_DIM // 2, -1.0, 1.0).astype(F32)
    cos = jnp.tile(jnp.cos(ang), (1, GROUP_HEADS))
    sin = jnp.tile(jnp.sin(ang) * sign[None, :], (1, GROUP_HEADS))
    return cos, sin


def _pad_rows(a, rows):
    return jnp.concatenate([a, jnp.zeros((rows - a.shape[0],) + a.shape[1:], a.dtype)], axis=0)


def kernel(x, mem, norm_mix, w_in, q_norm_a, k_norm_a, q_norm_b, k_norm_b, q_norm_m, k_norm_m,
           mem_norm, w_mem_kv, conv_w, conv_b, conv_ln_g, conv_ln_b, w_conv_out, out_norm, w_out,
           norm_ffn, w_up, ffn_conv_w, ffn_conv_b, w_down):
    b, seq, d = x.shape
    depth = w_in.shape[0]
    gw = GROUP_WIDTH
    cos, sin = _rope_tables(seq)
    x2 = x.reshape(b * seq, d)
    tile_h = lambda g: jnp.tile(g, GROUP_HEADS)
    r3 = lambda t: t.reshape(b, seq, gw)
    f2 = lambda t: t.reshape(b * seq, gw)
    w_in_b, w_mem_b, w_co_b = w_in.astype(BF16), w_mem_kv.astype(BF16), w_conv_out.astype(BF16)
    w_out_b, w_up_b, w_down_b = w_out.astype(BF16), w_up.astype(BF16), w_down.astype(BF16)
    for i in range(depth):
        gains = _pad_rows(jnp.stack([tile_h(q_norm_a[i]), tile_h(k_norm_a[i]), tile_h(q_norm_b[i]),
                                     tile_h(k_norm_b[i]), tile_h(q_norm_m[i])]), 8)
        gout = out_norm[i].reshape(4, 1, gw)
        qa, ka, vat, qbc, kb, kbc, vb, vbc, u, qm = _in_proj(x2, norm_mix[i][None], w_in_b, i, cos, sin, gains, seq)
        oa, om = _moba(r3(qa), r3(ka), vat, gout[0], r3(qm), mem, mem_norm[i][None], w_mem_b, i,
                       tile_h(k_norm_m[i])[None], gout[3])
        ob = _dilated(qbc, r3(kb), kbc, r3(vb), vbc, gout[1])
        oc = _conv_module(r3(u), _pad_rows(conv_w[i], CONV_PAD), conv_b[i][None], conv_ln_g[i][None],
                          conv_ln_b[i][None], w_co_b, i, gout[2])
        x2 = _ffn(x2, [f2(oa), f2(ob), f2(oc), f2(om)], w_out_b, norm_ffn[i][None], w_up_b,
                  _pad_rows(ffn_conv_w[i], 8), ffn_conv_b[i][None], w_down_b, i, seq)
    return x2.reshape(b, seq, d)
```
